```python
import jax, jax.numpy as jnp
from jax import lax
import numpy as np

D_MODEL = 1024
BATCH = 1
SEQ = 16384
DEPTH = 4
DEC_BATCH = 2
DEC_SEQ = 8192
PAST_LEN = 128

N_HEADS = 4
HEAD_DIM = 128
MLSTM_WIDTH = N_HEADS * HEAD_DIM
CONV_WIDTH = D_MODEL - MLSTM_WIDTH
CONV_K = 3
CHUNK = 64
N_DIR = 2
N_GATE_COLS = N_DIR * 2 * N_HEADS
PROJ_WIDTH = 4 * MLSTM_WIDTH + N_GATE_COLS + 3 * CONV_WIDTH
D_FF = 4 * D_MODEL
ALPHA = (2.0 * DEPTH) ** 0.25
BETA = (8.0 * DEPTH) ** -0.25
LN_EPS = 1e-5

kernel_name = "hymba_mlstm_shortconv_deepnorm_encoder"

_SPLITS = list(np.cumsum([MLSTM_WIDTH, MLSTM_WIDTH, MLSTM_WIDTH, MLSTM_WIDTH,
                          N_GATE_COLS, CONV_WIDTH, CONV_WIDTH]))


def layer_norm(x, g, b):
    xf = x.astype(jnp.float32)
    mu = jnp.mean(xf, axis=-1, keepdims=True)
    xc = xf - mu
    var = jnp.mean(xc * xc, axis=-1, keepdims=True)
    y = xc * lax.rsqrt(var + LN_EPS) * g.astype(jnp.float32) + b.astype(jnp.float32)
    return y.astype(x.dtype)


def mlstm_chunkwise(q, k, v, ig, lf):
    n_b, n_h, s_len, dh = q.shape
    nc = s_len // CHUNK

    def to_chunks(t):
        t = t.reshape(n_b, n_h, nc, CHUNK, *t.shape[3:])
        return jnp.moveaxis(t, 2, 0)

    qc, kc, vc, ic = to_chunks(q), to_chunks(k), to_chunks(v), to_chunks(ig)
    bc = jnp.cumsum(to_chunks(lf), axis=-1)
    causal_in_scan = jnp.tril(jnp.ones((CHUNK, CHUNK), dtype=bool))

    def step(carry, inp):
        C, n, m = carry
        qt, kt, vt, it, bt = inp
        g = bt[..., -1]
        D = bt[..., :, None] - bt[..., None, :] + it[..., None, :]
        D = jnp.where(causal_in_scan, D, -jnp.inf)
        inter = bt + m[..., None]
        m_t = jnp.maximum(inter, jnp.max(D, axis=-1))
        wD = jnp.exp(D - m_t[..., None])
        w_inter = jnp.exp(inter - m_t)
        s = jnp.einsum('nhtd,nhsd->nhts', qt, kt) * wD
        num = (jnp.einsum('nhts,nhse->nhte', s, vt)
               + w_inter[..., None] * jnp.einsum('nhed,nhtd->nhte', C, qt))
        den = jnp.sum(s, axis=-1) + w_inter * jnp.einsum('nhd,nhtd->nht', n, qt)
        h = num / jnp.maximum(jnp.abs(den), jnp.exp(-m_t))[..., None]
        a = g[..., None] - bt + it
        m_new = jnp.maximum(g + m, jnp.max(a, axis=-1))
        wa = jnp.exp(a - m_new[..., None])
        wc = jnp.exp(g + m - m_new)
        C_new = wc[..., None, None] * C + jnp.einsum('nhs,nhse,nhsd->nhed', wa, vt, kt)
        n_new = wc[..., None] * n + jnp.einsum('nhs,nhsd->nhd', wa, kt)
        return (C_new, n_new, m_new), h

    init = (jnp.zeros((n_b, n_h, dh, dh), jnp.float32),
            jnp.zeros((n_b, n_h, dh), jnp.float32),
            jnp.zeros((n_b, n_h), jnp.float32))
    _, hs = lax.scan(step, init, (qc, kc, vc, ic, bc))
    return jnp.moveaxis(hs, 0, 2).reshape(n_b, n_h, s_len, dh)


def token_mixer(x, w_in, b_gate, mh_norm_w, conv_w, w_out):
    bsz, s_len, _ = x.shape
    proj = x @ w_in
    q, k, v, o, gates, cb, cc, ch = jnp.split(proj, _SPLITS, axis=-1)

    def heads(t):
        return t.reshape(bsz, s_len, N_HEADS, HEAD_DIM).transpose(0, 2, 1, 3).astype(jnp.float32)

    def both_dirs(t):
        return jnp.concatenate([t, jnp.flip(t, axis=2)], axis=0)

    gt = gates.astype(jnp.float32).reshape(bsz, s_len, N_DIR, 2, N_HEADS) + b_gate.astype(jnp.float32)
    gt = gt.transpose(2, 3, 0, 4, 1)
    ig = jnp.concatenate([gt[0, 0], jnp.flip(gt[1, 0], axis=-1)], axis=0)
    lf = jax.nn.log_sigmoid(jnp.concatenate([gt[0, 1], jnp.flip(gt[1, 1], axis=-1)], axis=0))
    qh = heads(q) * (HEAD_DIM ** -0.5)
    h = mlstm_chunkwise(both_dirs(qh), both_dirs(heads(k)), both_dirs(heads(v)), ig, lf)
    h = h[:bsz] + jnp.flip(h[bsz:], axis=2)
    mu = jnp.mean(h, axis=-1, keepdims=True)
    hc = h - mu
    h = hc * lax.rsqrt(jnp.mean(hc * hc, axis=-1, keepdims=True) + LN_EPS)
    h = h.transpose(0, 2, 1, 3).reshape(bsz, s_len, MLSTM_WIDTH) * mh_norm_w.astype(jnp.float32)
    h_m = (jax.nn.sigmoid(o.astype(jnp.float32)) * h).astype(x.dtype)

    u = cc * ch
    y = lax.conv_general_dilated(
        u, conv_w[:, None, :].astype(u.dtype), window_strides=(1,),
        padding=((CONV_K // 2, CONV_K // 2),),
        dimension_numbers=('NWC', 'WIO', 'NWC'),
        feature_group_count=CONV_WIDTH)
    h_c = cb * y

    return jnp.concatenate([h_m, h_c], axis=-1) @ w_out


def trunk(x, w_in, b_gate, mh_norm_w, conv_w, w_out, ln1_g, ln1_b, w_ff1, w_ff2, ln2_g, ln2_b):
    for l in range(DEPTH):
        mix = token_mixer(x, w_in[l], b_gate[l], mh_norm_w[l], conv_w[l], w_out[l])
        x = layer_norm(ALPHA * x + mix, ln1_g[l], ln1_b[l])
        ff = jnp.square(jax.nn.relu(x @ w_ff1[l])) @ w_ff2[l]
        x = layer_norm(ALPHA * x + ff, ln2_g[l], ln2_b[l])
    return x


def setup_inputs(seed: int = 0) -> dict:
    key = jax.random.key(seed)
    ks = jax.random.split(key, 16)
    f32 = jnp.float32
    x_prompt = jax.random.normal(ks[0], (BATCH, SEQ, D_MODEL), f32)
    x_sample = jax.random.normal(ks[1], (DEC_BATCH, DEC_SEQ, D_MODEL), f32)
    w_in = jax.random.normal(ks[2], (DEPTH, D_MODEL, PROJ_WIDTH), f32) * D_MODEL ** -0.5
    b_i = 0.1 * jax.random.normal(ks[3], (DEPTH, N_DIR, N_HEADS), f32)
    b_f = jnp.linspace(3.0, 6.0, N_HEADS, dtype=f32) + 0.1 * jax.random.normal(ks[4], (DEPTH, N_DIR, N_HEADS), f32)
    b_gate = jnp.stack([b_i, b_f], axis=2)
    mh_norm_w = 1.0 + 0.02 * jax.random.normal(ks[5], (DEPTH, MLSTM_WIDTH), f32)
    conv_w = jax.random.normal(ks[6], (DEPTH, CONV_K, CONV_WIDTH), f32) * CONV_K ** -0.5
    w_out = jax.random.normal(ks[7], (DEPTH, D_MODEL, D_MODEL), f32) * (D_MODEL ** -0.5 * BETA)
    ln1_g = 1.0 + 0.02 * jax.random.normal(ks[8], (DEPTH, D_MODEL), f32)
    ln1_b = 0.02 * jax.random.normal(ks[9], (DEPTH, D_MODEL), f32)
    w_ff1 = jax.random.normal(ks[10], (DEPTH, D_MODEL, D_FF), f32) * D_MODEL ** -0.5
    w_ff2 = jax.random.normal(ks[11], (DEPTH, D_FF, D_MODEL), f32) * (D_FF ** -0.5 * BETA)
    ln2_g = 1.0 + 0.02 * jax.random.normal(ks[12], (DEPTH, D_MODEL), f32)
    ln2_b = 0.02 * jax.random.normal(ks[13], (DEPTH, D_MODEL), f32)
    return {"x_prompt": x_prompt, "x_sample": x_sample, "w_in": w_in, "b_gate": b_gate,
            "mh_norm_w": mh_norm_w, "conv_w": conv_w, "w_out": w_out,
            "ln1_g": ln1_g, "ln1_b": ln1_b, "w_ff1": w_ff1, "w_ff2": w_ff2,
            "ln2_g": ln2_g, "ln2_b": ln2_b}


def reference(x_prompt, x_sample, w_in, b_gate, mh_norm_w, conv_w, w_out,
              ln1_g, ln1_b, w_ff1, w_ff2, ln2_g, ln2_b):
    y_prompt = trunk(x_prompt, w_in, b_gate, mh_norm_w, conv_w, w_out,
                     ln1_g, ln1_b, w_ff1, w_ff2, ln2_g, ln2_b)
    y_sample = trunk(x_sample, w_in, b_gate, mh_norm_w, conv_w, w_out,
                     ln1_g, ln1_b, w_ff1, w_ff2, ln2_g, ln2_b)
    return (y_prompt, y_sample)
```

```python
import functools

import jax
import jax.numpy as jnp
from jax import lax
from jax.experimental import pallas as pl
from jax.experimental.pallas import tpu as pltpu

D_MODEL = 1024
DEPTH = 4
N_HEADS = 4
HEAD_DIM = 128
MLSTM_WIDTH = N_HEADS * HEAD_DIM
CONV_WIDTH = D_MODEL - MLSTM_WIDTH
N_DIR = 2
N_HD = N_DIR * N_HEADS
D_FF = 4 * D_MODEL
ALPHA = (2.0 * DEPTH) ** 0.25
LN_EPS = 1e-5
Q_SCALE = HEAD_DIM ** -0.5

MAIN_WIDTH = 4 * MLSTM_WIDTH + 3 * CONV_WIDTH
COL_BLOCK = 512
Q_BLK, K_BLK, V_BLK, O_BLK, CB_BLK, CC_BLK, CH_BLK = range(7)

CHUNK = 128
ROW_TILE = 512
FF_SPLIT = 4
HALO_ROWS = 16
VMEM_LIMIT = 56 * 1024 * 1024

F32 = jnp.float32
BF16 = jnp.bfloat16


def _const_spec(shape):
    nd = len(shape)
    return pl.BlockSpec(shape, lambda *_: (0,) * nd, pipeline_mode=pl.Buffered(1))


def _layer_norm(x, g, b):
    mu = jnp.mean(x, axis=-1, keepdims=True)
    xc = x - mu
    var = jnp.mean(xc * xc, axis=-1, keepdims=True)
    return xc * lax.rsqrt(var + LN_EPS) * g + b


def _log_sigmoid(x):
    return jnp.minimum(x, 0.0) - jnp.log1p(jnp.exp(-jnp.abs(x)))


def _proj_kernel(x_ref, wm_ref, wg_ref, wgt_ref, bias_c_ref, bias_r_ref,
                 proj_ref, ic_ref, bc_ref, ir_ref, br_ref):
    xb = x_ref[...].astype(BF16)
    p = jnp.dot(xb, wm_ref[...], preferred_element_type=F32)
    proj_ref[:, :MLSTM_WIDTH] = (p[:, :MLSTM_WIDTH] * Q_SCALE).astype(BF16)
    proj_ref[:, MLSTM_WIDTH:] = p[:, MLSTM_WIDTH:].astype(BF16)

    g_col = jnp.dot(xb, wg_ref[...], preferred_element_type=F32) + bias_c_ref[...]
    g_row = lax.dot_general(wgt_ref[...], xb, (((1,), (1,)), ((), ())),
                            preferred_element_type=F32) + bias_r_ref[...]
    ic_ref[...] = g_col[:, :N_HD]
    ir_ref[...] = g_row[:N_HD, :]
    lf_col = _log_sigmoid(g_col[:, N_HD:])
    lf_row = _log_sigmoid(g_row[N_HD:, :])

    r = lax.broadcasted_iota(jnp.int32, (CHUNK, CHUNK), 0)
    c = lax.broadcasted_iota(jnp.int32, (CHUNK, CHUNK), 1)
    lower = (c <= r).astype(F32)
    upper = (c >= r).astype(F32)
    fwd_cols = lax.broadcasted_iota(jnp.int32, (CHUNK, N_HD), 1) < N_HEADS
    fwd_rows = lax.broadcasted_iota(jnp.int32, (N_HD, CHUNK), 0) < N_HEADS
    hi = lax.Precision.HIGHEST
    for j in range(ROW_TILE // CHUNK):
        sl = slice(j * CHUNK, (j + 1) * CHUNK)
        lc = lf_col[sl, :]
        bc_ref[sl, :] = jnp.where(
            fwd_cols,
            jnp.dot(lower, lc, preferred_element_type=F32, precision=hi),
            jnp.dot(upper, lc, preferred_element_type=F32, precision=hi))
        lr = lf_row[:, sl]
        br_ref[:, sl] = jnp.where(
            fwd_rows,
            jnp.dot(lr, upper, preferred_element_type=F32, precision=hi),
            jnp.dot(lr, lower, preferred_element_type=F32, precision=hi))


def _proj_call(x, wm, wg, wgt, bias_c, bias_r):
    t = x.shape[0]
    grid = (t // ROW_TILE,)
    return pl.pallas_call(
        _proj_kernel,
        grid=grid,
        in_specs=[
            pl.BlockSpec((ROW_TILE, D_MODEL), lambda i: (i, 0)),
            _const_spec((D_MODEL, MAIN_WIDTH)),
            _const_spec((D_MODEL, 2 * N_HD)),
            _const_spec((2 * N_HD, D_MODEL)),
            _const_spec((1, 2 * N_HD)),
            _const_spec((2 * N_HD, 1)),
        ],
        out_specs=[
            pl.BlockSpec((ROW_TILE, MAIN_WIDTH), lambda i: (i, 0)),
            pl.BlockSpec((ROW_TILE, N_HD), lambda i: (i, 0)),
            pl.BlockSpec((ROW_TILE, N_HD), lambda i: (i, 0)),
            pl.BlockSpec((N_HD, ROW_TILE), lambda i: (0, i)),
            pl.BlockSpec((N_HD, ROW_TILE), lambda i: (0, i)),
        ],
        out_shape=[
            jax.ShapeDtypeStruct((t, MAIN_WIDTH), BF16),
            jax.ShapeDtypeStruct((t, N_HD), F32),
            jax.ShapeDtypeStruct((t, N_HD), F32),
            jax.ShapeDtypeStruct((N_HD, t), F32),
            jax.ShapeDtypeStruct((N_HD, t), F32),
        ],
        compiler_params=pltpu.CompilerParams(
            dimension_semantics=("parallel",), vmem_limit_bytes=VMEM_LIMIT),
        name="proj",
    )(x, wm, wg, wgt, bias_c, bias_r)


def _is_one_of(idx, values):
    hit = idx == values[0]
    for v in values[1:]:
        hit = jnp.logical_or(hit, idx == v)
    return hit


def _scan_direction(k_ref, v_ref, ic, bc, g, reset, s_scr, m_scr, st_out, m_out):
    m_prev = jnp.where(reset, 0.0, m_scr[0:1, 0:N_HEADS])
    a = g - bc + ic
    m_new = jnp.maximum(g + m_prev, jnp.max(a, axis=0, keepdims=True))
    wa = jnp.exp(a - m_new)
    wc = jnp.exp(g + m_prev - m_new)
    m_out[0] = m_prev
    m_scr[0:1, 0:N_HEADS] = m_new
    for h in range(N_HEADS):
        hs = slice(h * HEAD_DIM, (h + 1) * HEAD_DIM)
        s_prev = jnp.where(reset, 0.0, s_scr[h])
        st_out[0, h] = s_prev.astype(BF16)
        wah = wa[:, h:h + 1]
        v_aug = jnp.concatenate(
            [v_ref[:, hs].astype(F32) * wah, jnp.broadcast_to(wah, (CHUNK, HEAD_DIM))],
            axis=1).astype(BF16)
        upd = lax.dot_general(k_ref[:, hs], v_aug, (((0,), (0,)), ((), ())),
                              preferred_element_type=F32)
        s_scr[h] = wc[:, h:h + 1] * s_prev + upd


def _state_kernel(first_chunks, last_chunks,
                  kf_ref, vf_ref, icf_ref, bcf_ref, kb_ref, vb_ref, icb_ref, bcb_ref,
                  stf_ref, mf_ref, stb_ref, mb_ref,
                  sf_scr, sb_scr, mf_scr, mb_scr):
    c = pl.program_id(0)
    cb = pl.num_programs(0) - 1 - c

    @pl.when(c == 0)
    def _():
        sf_scr[...] = jnp.zeros_like(sf_scr)
        sb_scr[...] = jnp.zeros_like(sb_scr)
        mf_scr[...] = jnp.zeros_like(mf_scr)
        mb_scr[...] = jnp.zeros_like(mb_scr)

    bcf = bcf_ref[:, 0:N_HEADS]
    _scan_direction(kf_ref, vf_ref, icf_ref[:, 0:N_HEADS], bcf, bcf[CHUNK - 1:CHUNK, :],
                    _is_one_of(c, first_chunks), sf_scr, mf_scr, stf_ref, mf_ref)
    bcb = bcb_ref[:, N_HEADS:N_HD]
    _scan_direction(kb_ref, vb_ref, icb_ref[:, N_HEADS:N_HD], bcb, bcb[0:1, :],
                    _is_one_of(cb, last_chunks), sb_scr, mb_scr, stb_ref, mb_ref)


def _state_call(proj, ic, bc, first_chunks, last_chunks):
    t = proj.shape[0]
    n = t // CHUNK
    blk = lambda col, rev: pl.BlockSpec(
        (CHUNK, COL_BLOCK), (lambda i: (n - 1 - i, col)) if rev else (lambda i: (i, col)))
    gate = lambda rev: pl.BlockSpec(
        (CHUNK, N_HD), (lambda i: (n - 1 - i, 0)) if rev else (lambda i: (i, 0)))
    st_spec = lambda rev: pl.BlockSpec(
        (1, N_HEADS, HEAD_DIM, 2 * HEAD_DIM),
        (lambda i: (n - 1 - i, 0, 0, 0)) if rev else (lambda i: (i, 0, 0, 0)))
    m_spec = lambda rev: pl.BlockSpec(
        (1, 1, N_HEADS), (lambda i: (n - 1 - i, 0, 0)) if rev else (lambda i: (i, 0, 0)))
    st_shape = jax.ShapeDtypeStruct((n, N_HEADS, HEAD_DIM, 2 * HEAD_DIM), BF16)
    m_shape = jax.ShapeDtypeStruct((n, 1, N_HEADS), F32)
    return pl.pallas_call(
        functools.partial(_state_kernel, first_chunks, last_chunks),
        grid=(n,),
        in_specs=[blk(K_BLK, False), blk(V_BLK, False), gate(False), gate(False),
                  blk(K_BLK, True), blk(V_BLK, True), gate(True), gate(True)],
        out_specs=[st_spec(False), m_spec(False), st_spec(True), m_spec(True)],
        out_shape=[st_shape, m_shape, st_shape, m_shape],
        scratch_shapes=[
            pltpu.VMEM((N_HEADS, HEAD_DIM, 2 * HEAD_DIM), F32),
            pltpu.VMEM((N_HEADS, HEAD_DIM, 2 * HEAD_DIM), F32),
            pltpu.VMEM((8, 128), F32),
            pltpu.VMEM((8, 128), F32),
        ],
        compiler_params=pltpu.CompilerParams(
            dimension_semantics=("arbitrary",), vmem_limit_bytes=VMEM_LIMIT),
        name="state",
    )(proj, proj, ic, bc, proj, proj, ic, bc)


def _mixer_kernel(first_chunks, last_chunks,
                  q_ref, k_ref, v_ref, o_ref, cb_ref, cc_ref, ch_ref,
                  ccp_ref, chp_ref, ccn_ref, chn_ref,
                  ic_ref, bc_ref, ir_ref, br_ref,
                  stf_ref, mf_ref, stb_ref, mb_ref, nw_ref, cw_ref, out_ref):
    del ic_ref
    c = pl.program_id(0)
    row = lax.broadcasted_iota(jnp.int32, (CHUNK, CHUNK), 0)
    col = lax.broadcasted_iota(jnp.int32, (CHUNK, CHUNK), 1)
    masks = (col <= row, col >= row)
    st_refs = (stf_ref, stb_ref)
    m_refs = (mf_ref, mb_ref)
    ones = jnp.ones((CHUNK, HEAD_DIM), BF16)

    for h in range(N_HEADS):
        hs = slice(h * HEAD_DIM, (h + 1) * HEAD_DIM)
        qh = q_ref[:, hs]
        s_qk = lax.dot_general(qh, k_ref[:, hs], (((1,), (1,)), ((), ())),
                               preferred_element_type=F32)
        v_aug = jnp.concatenate([v_ref[:, hs], ones], axis=1)
        h_sum = None
        for d in range(N_DIR):
            j = d * N_HEADS + h
            r_row = ir_ref[j:j + 1, :] - br_ref[j:j + 1, :]
            dm = jnp.where(masks[d], r_row, -jnp.inf)
            m_prev = m_refs[d][0, :, h:h + 1]
            c_t = jnp.maximum(jnp.max(dm, axis=1, keepdims=True), m_prev)
            w_d = jnp.exp(dm - c_t)
            s_w = (s_qk * w_d).astype(BF16)
            intra = jnp.dot(s_w, v_aug, preferred_element_type=F32)
            inter = jnp.dot(qh, st_refs[d][0, h], preferred_element_type=F32)
            tot = intra + jnp.exp(m_prev - c_t) * inter
            floor = jnp.exp(-(bc_ref[:, j:j + 1] + c_t))
            den = jnp.maximum(jnp.abs(tot[:, HEAD_DIM:]), floor)
            h_dir = tot[:, :HEAD_DIM] / den
            h_sum = h_dir if h_sum is None else h_sum + h_dir
        mu = jnp.mean(h_sum, axis=-1, keepdims=True)
        hc = h_sum - mu
        hn = hc * lax.rsqrt(jnp.mean(hc * hc, axis=-1, keepdims=True) + LN_EPS)
        hn = hn * nw_ref[:, hs]
        out_ref[:, hs] = (jax.nn.sigmoid(o_ref[:, hs].astype(F32)) * hn).astype(BF16)

    u = cc_ref[...].astype(F32) * ch_ref[...].astype(F32)
    u_before = (ccp_ref[HALO_ROWS - 1:HALO_ROWS, :].astype(F32)
                * chp_ref[HALO_ROWS - 1:HALO_ROWS, :].astype(F32))
    u_after = ccn_ref[0:1, :].astype(F32) * chn_ref[0:1, :].astype(F32)
    u_before = jnp.where(_is_one_of(c, first_chunks), 0.0, u_before)
    u_after = jnp.where(_is_one_of(c, last_chunks), 0.0, u_after)
    rows = lax.broadcasted_iota(jnp.int32, (CHUNK, CONV_WIDTH), 0)
    u_prev = jnp.where(rows == 0, u_before, pltpu.roll(u, 1, axis=0))
    u_next = jnp.where(rows == CHUNK - 1, u_after, pltpu.roll(u, CHUNK - 1, axis=0))
    y = cw_ref[0:1, :] * u_prev + cw_ref[1:2, :] * u + cw_ref[2:3, :] * u_next
    out_ref[:, MLSTM_WIDTH:] = (cb_ref[...].astype(F32) * y).astype(BF16)


def _mixer_call(proj, ic, bc, ir, br, stf, mf, stb, mb, norm_w, conv_w, first_chunks, last_chunks):
    t = proj.shape[0]
    n = t // CHUNK
    per_halo = CHUNK // HALO_ROWS
    n_halo = t // HALO_ROWS
    blk = lambda col: pl.BlockSpec((CHUNK, COL_BLOCK), lambda i: (i, col))
    halo_prev = lambda col: pl.BlockSpec(
        (HALO_ROWS, COL_BLOCK), lambda i: (jnp.maximum(i * per_halo - 1, 0), col))
    halo_next = lambda col: pl.BlockSpec(
        (HALO_ROWS, COL_BLOCK), lambda i: (jnp.minimum((i + 1) * per_halo, n_halo - 1), col))
    col_gate = pl.BlockSpec((CHUNK, N_HD), lambda i: (i, 0))
    row_gate = pl.BlockSpec((N_HD, CHUNK), lambda i: (0, i))
    st_spec = pl.BlockSpec((1, N_HEADS, HEAD_DIM, 2 * HEAD_DIM), lambda i: (i, 0, 0, 0))
    m_spec = pl.BlockSpec((1, 1, N_HEADS), lambda i: (i, 0, 0))
    return pl.pallas_call(
        functools.partial(_mixer_kernel, first_chunks, last_chunks),
        grid=(n,),
        in_specs=[blk(Q_BLK), blk(K_BLK), blk(V_BLK), blk(O_BLK), blk(CB_BLK), blk(CC_BLK), blk(CH_BLK),
                  halo_prev(CC_BLK), halo_prev(CH_BLK), halo_next(CC_BLK), halo_next(CH_BLK),
                  col_gate, col_gate, row_gate, row_gate,
                  st_spec, m_spec, st_spec, m_spec,
                  _const_spec((1, MLSTM_WIDTH)), _const_spec((3, CONV_WIDTH))],
        out_specs=pl.BlockSpec((CHUNK, D_MODEL), lambda i: (i, 0)),
        out_shape=jax.ShapeDtypeStruct((t, D_MODEL), BF16),
        compiler_params=pltpu.CompilerParams(
            dimension_semantics=("parallel",), vmem_limit_bytes=VMEM_LIMIT),
        name="mixer",
    )(proj, proj, proj, proj, proj, proj, proj, proj, proj, proj, proj,
      ic, bc, ir, br, stf, mf, stb, mb, norm_w, conv_w)


def _dense_kernel(x_ref, mix_ref, wo_ref, g1_ref, b1_ref, w1_ref, w2_ref, g2_ref, b2_ref, out_ref):
    y = jnp.dot(mix_ref[...], wo_ref[...], preferred_element_type=F32)
    x1 = _layer_norm(ALPHA * x_ref[...] + y, g1_ref[...], b1_ref[...])
    x1b = x1.astype(BF16)
    acc = ALPHA * x1
    slab = D_FF // FF_SPLIT
    for j in range(FF_SPLIT):
        hid = jnp.dot(x1b, w1_ref[:, j * slab:(j + 1) * slab], preferred_element_type=F32)
        hid = jnp.square(jnp.maximum(hid, 0.0)).astype(BF16)
        acc = acc + jnp.dot(hid, w2_ref[j * slab:(j + 1) * slab, :], preferred_element_type=F32)
    out_ref[...] = _layer_norm(acc, g2_ref[...], b2_ref[...])


def _dense_call(x, mix, wo, g1, b1, w1, w2, g2, b2):
    t = x.shape[0]
    row = lambda: pl.BlockSpec((ROW_TILE, D_MODEL), lambda i: (i, 0))
    vec = lambda: _const_spec((1, D_MODEL))
    return pl.pallas_call(
        _dense_kernel,
        grid=(t // ROW_TILE,),
        in_specs=[row(), row(), _const_spec((D_MODEL, D_MODEL)), vec(), vec(),
                  _const_spec((D_MODEL, D_FF)), _const_spec((D_FF, D_MODEL)), vec(), vec()],
        out_specs=row(),
        out_shape=jax.ShapeDtypeStruct((t, D_MODEL), F32),
        compiler_params=pltpu.CompilerParams(
            dimension_semantics=("parallel",), vmem_limit_bytes=VMEM_LIMIT),
        name="dense",
    )(x, mix, wo, g1, b1, w1, w2, g2, b2)


def kernel(x_prompt, x_sample, w_in, b_gate, mh_norm_w, conv_w, w_out, ln1_g, ln1_b, w_ff1, w_ff2, ln2_g, ln2_b):
    seq_lens = [x_prompt.shape[1]] * x_prompt.shape[0] + [x_sample.shape[1]] * x_sample.shape[0]
    assert all(s % ROW_TILE == 0 and s % CHUNK == 0 for s in seq_lens)
    starts = [sum(seq_lens[:i]) for i in range(len(seq_lens))]
    first_chunks = tuple(s // CHUNK for s in starts)
    last_chunks = tuple((s + n) // CHUNK - 1 for s, n in zip(starts, seq_lens))

    x = jnp.concatenate([x_prompt.reshape(-1, D_MODEL), x_sample.reshape(-1, D_MODEL)], axis=0)

    gate_lo = 4 * MLSTM_WIDTH
    gate_hi = gate_lo + 2 * N_HD
    for l in range(DEPTH):
        w_main = jnp.concatenate([w_in[l, :, :gate_lo], w_in[l, :, gate_hi:]], axis=1).astype(BF16)
        w_gate = (w_in[l, :, gate_lo:gate_hi].reshape(D_MODEL, N_DIR, 2, N_HEADS)
                  .transpose(0, 2, 1, 3).reshape(D_MODEL, 2 * N_HD).astype(BF16))
        bias = b_gate[l].astype(F32).transpose(1, 0, 2).reshape(2 * N_HD)
        proj, ic, bc, ir, br = _proj_call(x, w_main, w_gate, w_gate.T,
                                          bias.reshape(1, -1), bias.reshape(-1, 1))
        stf, mf, stb, mb = _state_call(proj, ic, bc, first_chunks, last_chunks)
        mix = _mixer_call(proj, ic, bc, ir, br, stf, mf, stb, mb,
                          mh_norm_w[l].astype(F32).reshape(1, -1), conv_w[l].astype(F32),
                          first_chunks, last_chunks)
        x = _dense_call(x, mix, w_out[l].astype(BF16),
                        ln1_g[l].reshape(1, -1), ln1_b[l].reshape(1, -1),
                        w_ff1[l].astype(BF16), w_ff2[l].astype(BF16),
                        ln2_g[l].reshape(1, -1), ln2_b[l].reshape(1, -1))

    n_prompt = x_prompt.shape[0] * x_prompt.shape[1]
    return (x[:n_prompt].reshape(x_prompt.shape), x[n_prompt:].reshape(x_sample.shape))
```

```python
import functools

import jax
import jax.numpy as jnp
from jax import lax
from jax.experimental import pallas as pl
from jax.experimental.pallas import tpu as pltpu

D_MODEL = 1024
DEPTH = 4
N_HEADS = 4
HEAD_DIM = 128
MLSTM_WIDTH = N_HEADS * HEAD_DIM
CONV_WIDTH = D_MODEL - MLSTM_WIDTH
N_DIR = 2
N_HD = N_DIR * N_HEADS
D_FF = 4 * D_MODEL
ALPHA = (2.0 * DEPTH) ** 0.25
LN_EPS = 1e-5
Q_SCALE = HEAD_DIM ** -0.5

LANES = 128
COL_BLOCK = 512
Q_BLK, V_BLK, O_BLK, CB_BLK, CC_BLK, CH_BLK = range(6)
PM_WIDTH = 6 * COL_BLOCK
GATE_PAD = LANES

CHUNK = 128
ROW_TILE = 512
N_SUB = ROW_TILE // CHUNK
FF_SPLIT = 4
HALO_ROWS = 16
VMEM_LIMIT = 56 * 1024 * 1024

R_ROW, A_ROW, G_ROW, AMAX_ROW = 0, N_HD, 2 * N_HD, 3 * N_HD
GROW_ROWS = 4 * N_HD
C_TERMS = 2
B_TERMS = 3
GCOL_COLS = (C_TERMS + B_TERMS) * N_HD

F32 = jnp.float32
BF16 = jnp.bfloat16


def _const_spec(shape):
    nd = len(shape)
    return pl.BlockSpec(shape, lambda *_: (0,) * nd, pipeline_mode=pl.Buffered(1))


def _layer_norm(x, g, b):
    mu = jnp.mean(x, axis=-1, keepdims=True)
    xc = x - mu
    var = jnp.mean(xc * xc, axis=-1, keepdims=True)
    return xc * lax.rsqrt(var + LN_EPS) * g + b


def _log_sigmoid(x):
    return jnp.minimum(x, 0.0) - jnp.log1p(jnp.exp(-jnp.abs(x)))


def _is_one_of(idx, values):
    hit = idx == values[0]
    for v in values[1:]:
        hit = jnp.logical_or(hit, idx == v)
    return hit


def _chunk_scan(x, combine, neutral, fwd_row, pos):
    width = x.shape[1]
    s = 1
    while s < CHUNK:
        from_left = pltpu.roll(x, s, axis=1)
        from_right = pltpu.roll(x, width - s, axis=1)
        take_left = jnp.logical_and(fwd_row, pos >= s)
        take_right = jnp.logical_and(jnp.logical_not(fwd_row), pos < CHUNK - s)
        x = combine(x, jnp.where(take_left, from_left, jnp.where(take_right, from_right, neutral)))
        s *= 2
    return x


def _proj_kernel(x_ref, wm_ref, wkt_ref, bias_ref, pm_ref, kt_ref, grow_ref, gcol_ref):
    xb = x_ref[...].astype(BF16)
    p = jnp.dot(xb, wm_ref[...], preferred_element_type=F32)
    pm_ref[:, :COL_BLOCK] = (p[:, :COL_BLOCK] * Q_SCALE).astype(BF16)
    pm_ref[:, COL_BLOCK:] = p[:, COL_BLOCK:PM_WIDTH].astype(BF16)
    kt_ref[...] = lax.dot_general(wkt_ref[...], xb, (((1,), (1,)), ((), ())),
                                  preferred_element_type=F32).astype(BF16)

    g_row = p[:, PM_WIDTH:].T[:2 * N_HD, :] + bias_ref[...]
    ig = g_row[:N_HD, :]
    lf = _log_sigmoid(g_row[N_HD:, :])
    fwd_row = lax.broadcasted_iota(jnp.int32, (N_HD, ROW_TILE), 0) < N_HEADS
    pos = jnp.bitwise_and(lax.broadcasted_iota(jnp.int32, (N_HD, ROW_TILE), 1), CHUNK - 1)
    b = _chunk_scan(lf, jnp.add, 0.0, fwd_row, pos)
    r = ig - b
    cm = _chunk_scan(r, jnp.maximum, -jnp.inf, fwd_row, pos)
    at_end = pos == jnp.where(fwd_row, CHUNK - 1, 0)
    b_end = jnp.where(at_end, b, 0.0)
    g_parts, amax_parts = [], []
    for j in range(N_SUB):
        sl = slice(j * CHUNK, (j + 1) * CHUNK)
        g_j = jnp.sum(b_end[:, sl], axis=1, keepdims=True)
        amax_j = g_j + jnp.max(r[:, sl], axis=1, keepdims=True)
        g_parts.append(jnp.broadcast_to(g_j, (N_HD, CHUNK)))
        amax_parts.append(jnp.broadcast_to(amax_j, (N_HD, CHUNK)))
    g_rep = jnp.concatenate(g_parts, axis=1)
    grow_ref[R_ROW:R_ROW + N_HD, :] = r
    grow_ref[A_ROW:A_ROW + N_HD, :] = g_rep + r
    grow_ref[G_ROW:G_ROW + N_HD, :] = g_rep
    grow_ref[AMAX_ROW:AMAX_ROW + N_HD, :] = jnp.concatenate(amax_parts, axis=1)

    terms, rest = [], b
    for _ in range(B_TERMS):
        term = rest.astype(BF16).astype(F32)
        terms.append(term)
        rest = rest - term
    col_src = jnp.concatenate(
        [cm] * C_TERMS + terms + [jnp.zeros((LANES - GCOL_COLS, ROW_TILE), F32)], axis=0)
    gcol_ref[...] = col_src.T[:, :GCOL_COLS]


def _proj_call(x_parts, wm, wkt, bias):
    t = sum(xp.shape[0] for xp in x_parts)
    n_first = x_parts[0].shape[0] // ROW_TILE
    if len(x_parts) == 1:
        x_specs = [pl.BlockSpec((ROW_TILE, D_MODEL), lambda i: (i, 0))]
        body = _proj_kernel
    else:
        x_specs = [pl.BlockSpec((ROW_TILE, D_MODEL), lambda i: (jnp.minimum(i, n_first - 1), 0)),
                   pl.BlockSpec((ROW_TILE, D_MODEL), lambda i: (jnp.maximum(i - n_first, 0), 0))]
        body = functools.partial(_two_source, _proj_kernel, n_first)
    return pl.pallas_call(
        body,
        grid=(t // ROW_TILE,),
        in_specs=x_specs + [
            _const_spec((D_MODEL, PM_WIDTH + GATE_PAD)),
            _const_spec((COL_BLOCK, D_MODEL)),
            _const_spec((2 * N_HD, 1)),
        ],
        out_specs=[
            pl.BlockSpec((ROW_TILE, PM_WIDTH), lambda i: (i, 0)),
            pl.BlockSpec((COL_BLOCK, ROW_TILE), lambda i: (0, i)),
            pl.BlockSpec((GROW_ROWS, ROW_TILE), lambda i: (0, i)),
            pl.BlockSpec((ROW_TILE, GCOL_COLS), lambda i: (i, 0)),
        ],
        out_shape=[
            jax.ShapeDtypeStruct((t, PM_WIDTH), BF16),
            jax.ShapeDtypeStruct((COL_BLOCK, t), BF16),
            jax.ShapeDtypeStruct((GROW_ROWS, t), F32),
            jax.ShapeDtypeStruct((t, GCOL_COLS), F32),
        ],
        compiler_params=pltpu.CompilerParams(
            dimension_semantics=("parallel",), vmem_limit_bytes=VMEM_LIMIT),
        name="proj",
    )(*x_parts, wm, wkt, bias)


class _PickedRef:
    def __init__(self, first, second, use_first):
        self._first, self._second, self._use_first = first, second, use_first

    def __getitem__(self, idx):
        return jnp.where(self._use_first, self._first[idx], self._second[idx])


def _two_source(body, n_first, xa_ref, xb_ref, *rest):
    body(_PickedRef(xa_ref, xb_ref, pl.program_id(0) < n_first), *rest)


def _state_kernel(first_chunks, last_chunks,
                  ktf_ref, vf_ref, gf_ref, ktb_ref, vb_ref, gb_ref,
                  stf_ref, mf_ref, stb_ref, mb_ref,
                  s_scr, m_scr):
    i = pl.program_id(0)
    n_tiles = pl.num_programs(0)

    @pl.when(i == 0)
    def _():
        s_scr[...] = jnp.zeros_like(s_scr)
        m_scr[...] = jnp.zeros_like(m_scr)

    pair_row = lax.broadcasted_iota(jnp.int32, (N_HD, CHUNK), 0)
    fwd_row = pair_row < N_HEADS
    diag = pair_row == jnp.bitwise_and(lax.broadcasted_iota(jnp.int32, (N_HD, CHUNK), 1), N_HD - 1)
    ones = jnp.ones((CHUNK, HEAD_DIM), BF16)
    kt_refs, v_refs, st_refs = (ktf_ref, ktb_ref), (vf_ref, vb_ref), (stf_ref, stb_ref)

    for step in range(N_SUB):
        sub = (step, N_SUB - 1 - step)
        lanes = tuple(slice(s * CHUNK, (s + 1) * CHUNK) for s in sub)
        chunk_f = i * N_SUB + sub[0]
        chunk_b = (n_tiles - 1 - i) * N_SUB + sub[1]
        reset = (_is_one_of(chunk_f, first_chunks), _is_one_of(chunk_b, last_chunks))

        def both(row0):
            return jnp.where(fwd_row, gf_ref[row0:row0 + N_HD, lanes[0]],
                             gb_ref[row0:row0 + N_HD, lanes[1]])

        a, g, amax = both(A_ROW), both(G_ROW), both(AMAX_ROW)
        m_old = m_scr[...]
        m_prev = jnp.where(fwd_row, jnp.where(reset[0], 0.0, m_old), jnp.where(reset[1], 0.0, m_old))
        m_new = jnp.maximum(g + m_prev, amax)
        wa = jnp.exp(a - m_new)
        wc = jnp.exp(g + m_prev - m_new)
        m_scr[...] = m_new
        m_lane = jnp.sum(jnp.where(diag, m_prev, 0.0), axis=0, keepdims=True)
        m_both = jnp.concatenate([m_prev, jnp.broadcast_to(m_lane, (N_HD, CHUNK))], axis=0)
        mf_ref[sub[0]] = m_both
        mb_ref[sub[1]] = m_both

        for d in range(N_DIR):
            for h in range(N_HEADS):
                j = d * N_HEADS + h
                hs = slice(h * HEAD_DIM, (h + 1) * HEAD_DIM)
                rows = slice(sub[d] * CHUNK, (sub[d] + 1) * CHUNK)
                s_prev = jnp.where(reset[d], 0.0, s_scr[j])
                st_refs[d][sub[d], h] = s_prev.astype(BF16)
                kt_w = (kt_refs[d][hs, lanes[d]].astype(F32) * wa[j:j + 1, :]).astype(BF16)
                v_aug = jnp.concatenate([v_refs[d][rows, hs], ones], axis=1)
                upd = jnp.dot(kt_w, v_aug, preferred_element_type=F32)
                wc_j = wc[j:j + 1, :]
                s_scr[j] = jnp.concatenate([wc_j, wc_j], axis=1) * s_prev + upd


def _state_call(pm, kt, grow, first_chunks, last_chunks):
    t = pm.shape[0]
    n = t // ROW_TILE
    n_chunks = t // CHUNK
    rev = lambda i: n - 1 - i
    st_block = (N_SUB, N_HEADS, HEAD_DIM, 2 * HEAD_DIM)
    m_block = (N_SUB, 2 * N_HD, CHUNK)
    st_shape = jax.ShapeDtypeStruct((n_chunks, N_HEADS, HEAD_DIM, 2 * HEAD_DIM), BF16)
    m_shape = jax.ShapeDtypeStruct((n_chunks, 2 * N_HD, CHUNK), F32)
    return pl.pallas_call(
        functools.partial(_state_kernel, first_chunks, last_chunks),
        grid=(n,),
        in_specs=[
            pl.BlockSpec((COL_BLOCK, ROW_TILE), lambda i: (0, i)),
            pl.BlockSpec((ROW_TILE, COL_BLOCK), lambda i: (i, V_BLK)),
            pl.BlockSpec((GROW_ROWS, ROW_TILE), lambda i: (0, i)),
            pl.BlockSpec((COL_BLOCK, ROW_TILE), lambda i: (0, rev(i))),
            pl.BlockSpec((ROW_TILE, COL_BLOCK), lambda i: (rev(i), V_BLK)),
            pl.BlockSpec((GROW_ROWS, ROW_TILE), lambda i: (0, rev(i))),
        ],
        out_specs=[
            pl.BlockSpec(st_block, lambda i: (i, 0, 0, 0)),
            pl.BlockSpec(m_block, lambda i: (i, 0, 0)),
            pl.BlockSpec(st_block, lambda i: (rev(i), 0, 0, 0)),
            pl.BlockSpec(m_block, lambda i: (rev(i), 0, 0)),
        ],
        out_shape=[st_shape, m_shape, st_shape, m_shape],
        scratch_shapes=[
            pltpu.VMEM((N_HD, HEAD_DIM, 2 * HEAD_DIM), F32),
            pltpu.VMEM((N_HD, CHUNK), F32),
        ],
        compiler_params=pltpu.CompilerParams(
            dimension_semantics=("arbitrary",), vmem_limit_bytes=VMEM_LIMIT),
        name="state",
    )(kt, pm, grow, kt, pm, grow)


def _mixer_kernel(first_chunks, last_chunks,
                  q_ref, v_ref, o_ref, cb_ref, cc_ref, ch_ref, kt_ref,
                  ccp_ref, chp_ref, ccn_ref, chn_ref,
                  r_ref, gcol_ref, stf_ref, mf_ref, stb_ref, mb_ref, spread_ref, nw_ref, cw_ref, out_ref):
    i = pl.program_id(0)
    row = lax.broadcasted_iota(jnp.int32, (CHUNK, CHUNK), 0)
    col = lax.broadcasted_iota(jnp.int32, (CHUNK, CHUNK), 1)
    masks = (col <= row, col >= row)
    st_refs = (stf_ref, stb_ref)
    m_refs = (mf_ref, mb_ref)
    ones = jnp.ones((CHUNK, HEAD_DIM), BF16)
    gate_lane = lax.broadcasted_iota(jnp.int32, (CHUNK, GCOL_COLS), 1)
    pair_lane = lax.broadcasted_iota(jnp.int32, (1, GCOL_COLS), 1)

    for sub in range(N_SUB):
        rows = slice(sub * CHUNK, (sub + 1) * CHUNK)
        m_lane = jnp.where(jnp.bitwise_and(pair_lane, N_HD - 1) < N_HEADS,
                           mf_ref[sub, N_HD:N_HD + 1, :GCOL_COLS],
                           mb_ref[sub, N_HD:N_HD + 1, :GCOL_COLS])
        gates = gcol_ref[rows, :]
        c_t = jnp.maximum(gates, m_lane)
        c_hi = c_t.astype(BF16).astype(F32)
        gates = jnp.where(gate_lane < N_HD, c_hi,
                          jnp.where(gate_lane < C_TERMS * N_HD, c_t - c_hi, gates)).astype(BF16)
        for h in range(N_HEADS):
            hs = slice(h * HEAD_DIM, (h + 1) * HEAD_DIM)
            qh = q_ref[rows, hs]
            qh32 = qh.astype(F32)
            s_qk = jnp.dot(qh, kt_ref[hs, rows], preferred_element_type=F32)
            v_aug = jnp.concatenate([v_ref[rows, hs], ones], axis=1)
            h_sum = None
            for d in range(N_DIR):
                j = d * N_HEADS + h
                spread = jnp.dot(gates, spread_ref[j], preferred_element_type=F32)
                c_b = spread[:, :CHUNK]
                m_prev = m_refs[d][sub, j:j + 1, :]
                w_d = jnp.exp(jnp.where(masks[d], r_ref[j:j + 1, rows] - c_b, -jnp.inf))
                s_w = (s_qk * w_d).astype(BF16)
                q_w = (qh32 * jnp.exp(m_prev - c_b)).astype(BF16)
                tot = jnp.dot(jnp.concatenate([s_w, q_w], axis=1),
                              jnp.concatenate([v_aug, st_refs[d][sub, h]], axis=0),
                              preferred_element_type=F32)
                floor = jnp.exp(-spread[:, CHUNK:])
                h_dir = tot[:, :HEAD_DIM] / jnp.maximum(jnp.abs(tot[:, HEAD_DIM:]), floor)
                h_sum = h_dir if h_sum is None else h_sum + h_dir
            mu = jnp.mean(h_sum, axis=-1, keepdims=True)
            hc = h_sum - mu
            hn = hc * lax.rsqrt(jnp.mean(hc * hc, axis=-1, keepdims=True) + LN_EPS)
            hn = hn * nw_ref[:, hs]
            out_ref[rows, hs] = (jax.nn.sigmoid(o_ref[rows, hs].astype(F32)) * hn).astype(BF16)

    first_chunk = i * N_SUB
    last_chunk = first_chunk + N_SUB - 1
    u = cc_ref[...].astype(F32) * ch_ref[...].astype(F32)
    u_before = (ccp_ref[HALO_ROWS - 1:HALO_ROWS, :].astype(F32)
                * chp_ref[HALO_ROWS - 1:HALO_ROWS, :].astype(F32))
    u_after = ccn_ref[0:1, :].astype(F32) * chn_ref[0:1, :].astype(F32)
    u_before = jnp.where(_is_one_of(first_chunk, first_chunks), 0.0, u_before)
    u_after = jnp.where(_is_one_of(last_chunk, last_chunks), 0.0, u_after)
    tok = lax.broadcasted_iota(jnp.int32, (ROW_TILE, CONV_WIDTH), 0)
    u_prev = jnp.where(tok == 0, u_before, pltpu.roll(u, 1, axis=0))
    u_next = jnp.where(tok == ROW_TILE - 1, u_after, pltpu.roll(u, ROW_TILE - 1, axis=0))
    y = cw_ref[0:1, :] * u_prev + cw_ref[1:2, :] * u + cw_ref[2:3, :] * u_next
    out_ref[:, MLSTM_WIDTH:] = (cb_ref[...].astype(F32) * y).astype(BF16)


def _mixer_call(pm, kt, grow, gcol, stf, mf, stb, mb, norm_w, conv_w, first_chunks, last_chunks):
    t = pm.shape[0]
    n = t // ROW_TILE
    per_halo = ROW_TILE // HALO_ROWS
    n_halo = t // HALO_ROWS
    blk = lambda col: pl.BlockSpec((ROW_TILE, COL_BLOCK), lambda i: (i, col))
    halo_prev = lambda col: pl.BlockSpec(
        (HALO_ROWS, COL_BLOCK), lambda i: (jnp.maximum(i * per_halo - 1, 0), col))
    halo_next = lambda col: pl.BlockSpec(
        (HALO_ROWS, COL_BLOCK), lambda i: (jnp.minimum((i + 1) * per_halo, n_halo - 1), col))
    st_spec = pl.BlockSpec((N_SUB, N_HEADS, HEAD_DIM, 2 * HEAD_DIM), lambda i: (i, 0, 0, 0))
    m_spec = pl.BlockSpec((N_SUB, 2 * N_HD, CHUNK), lambda i: (i, 0, 0))
    return pl.pallas_call(
        functools.partial(_mixer_kernel, first_chunks, last_chunks),
        grid=(n,),
        in_specs=[blk(Q_BLK), blk(V_BLK), blk(O_BLK), blk(CB_BLK), blk(CC_BLK), blk(CH_BLK),
                  pl.BlockSpec((COL_BLOCK, ROW_TILE), lambda i: (0, i)),
                  halo_prev(CC_BLK), halo_prev(CH_BLK), halo_next(CC_BLK), halo_next(CH_BLK),
                  pl.BlockSpec((N_HD, ROW_TILE), lambda i: (0, i)),
                  pl.BlockSpec((ROW_TILE, GCOL_COLS), lambda i: (i, 0)),
                  st_spec, m_spec, st_spec, m_spec,
                  _const_spec((N_HD, GCOL_COLS, 2 * CHUNK)),
                  _const_spec((1, MLSTM_WIDTH)), _const_spec((3, CONV_WIDTH))],
        out_specs=pl.BlockSpec((ROW_TILE, D_MODEL), lambda i: (i, 0)),
        out_shape=jax.ShapeDtypeStruct((t, D_MODEL), BF16),
        compiler_params=pltpu.CompilerParams(
            dimension_semantics=("parallel",), vmem_limit_bytes=VMEM_LIMIT),
        name="mixer",
    )(pm, pm, pm, pm, pm, pm, kt, pm, pm, pm, pm, grow, gcol, stf, mf, stb, mb,
      _spread_matrix(), norm_w, conv_w)


def _spread_matrix():
    term = jnp.arange(GCOL_COLS)[None, :, None] // N_HD
    pair = jnp.arange(GCOL_COLS)[None, :, None] % N_HD
    want = jnp.arange(N_HD)[:, None, None]
    lane = jnp.arange(2 * CHUNK)[None, None, :]
    hit = (pair == want) & ((term < C_TERMS) | (lane >= CHUNK))
    return hit.astype(BF16)


def _dense_kernel(x_ref, mix_ref, wo_ref, g1_ref, b1_ref, w1_ref, w2_ref, g2_ref, b2_ref, out_ref):
    y = jnp.dot(mix_ref[...], wo_ref[...], preferred_element_type=F32)
    x1 = _layer_norm(ALPHA * x_ref[...] + y, g1_ref[...], b1_ref[...])
    x1b = x1.astype(BF16)
    acc = ALPHA * x1
    slab = D_FF // FF_SPLIT
    for j in range(FF_SPLIT):
        hid = jnp.dot(x1b, w1_ref[:, j * slab:(j + 1) * slab], preferred_element_type=F32)
        hid = jnp.square(jnp.maximum(hid, 0.0)).astype(BF16)
        acc = acc + jnp.dot(hid, w2_ref[j * slab:(j + 1) * slab, :], preferred_element_type=F32)
    out_ref[...] = _layer_norm(acc, g2_ref[...], b2_ref[...])


def _dense_call(x_parts, mix, wo, g1, b1, w1, w2, g2, b2):
    t = mix.shape[0]
    n_first = x_parts[0].shape[0] // ROW_TILE
    row = lambda: pl.BlockSpec((ROW_TILE, D_MODEL), lambda i: (i, 0))
    vec = lambda: _const_spec((1, D_MODEL))
    if len(x_parts) == 1:
        x_specs = [row()]
        body = _dense_kernel
    else:
        x_specs = [pl.BlockSpec((ROW_TILE, D_MODEL), lambda i: (jnp.minimum(i, n_first - 1), 0)),
                   pl.BlockSpec((ROW_TILE, D_MODEL), lambda i: (jnp.maximum(i - n_first, 0), 0))]
        body = functools.partial(_two_source, _dense_kernel, n_first)
    return pl.pallas_call(
        body,
        grid=(t // ROW_TILE,),
        in_specs=x_specs + [row(), _const_spec((D_MODEL, D_MODEL)), vec(), vec(),
                            _const_spec((D_MODEL, D_FF)), _const_spec((D_FF, D_MODEL)), vec(), vec()],
        out_specs=row(),
        out_shape=jax.ShapeDtypeStruct((t, D_MODEL), F32),
        compiler_params=pltpu.CompilerParams(
            dimension_semantics=("parallel",), vmem_limit_bytes=VMEM_LIMIT),
        name="dense",
    )(*x_parts, mix, wo, g1, b1, w1, w2, g2, b2)


def kernel(x_prompt, x_sample, w_in, b_gate, mh_norm_w, conv_w, w_out, ln1_g, ln1_b, w_ff1, w_ff2, ln2_g, ln2_b):
    seq_lens = [x_prompt.shape[1]] * x_prompt.shape[0] + [x_sample.shape[1]] * x_sample.shape[0]
    assert all(s % ROW_TILE == 0 for s in seq_lens)
    starts = [sum(seq_lens[:i]) for i in range(len(seq_lens))]
    first_chunks = tuple(s // CHUNK for s in starts)
    last_chunks = tuple((s + n) // CHUNK - 1 for s, n in zip(starts, seq_lens))

    x_parts = (x_prompt.reshape(-1, D_MODEL), x_sample.reshape(-1, D_MODEL))

    k_lo, v_lo = MLSTM_WIDTH, 2 * MLSTM_WIDTH
    gate_lo = 4 * MLSTM_WIDTH
    gate_hi = gate_lo + 2 * N_HD
    for l in range(DEPTH):
        w_gate = (w_in[l, :, gate_lo:gate_hi].reshape(D_MODEL, N_DIR, 2, N_HEADS)
                  .transpose(0, 2, 1, 3).reshape(D_MODEL, 2 * N_HD))
        w_main = jnp.concatenate(
            [w_in[l, :, :k_lo], w_in[l, :, v_lo:gate_lo], w_in[l, :, gate_hi:],
             w_gate, jnp.zeros((D_MODEL, GATE_PAD - 2 * N_HD), w_in.dtype)], axis=1).astype(BF16)
        w_kt = w_in[l, :, k_lo:v_lo].T.astype(BF16)
        bias = b_gate[l].astype(F32).transpose(1, 0, 2).reshape(2 * N_HD, 1)
        pm, kt, grow, gcol = _proj_call(x_parts, w_main, w_kt, bias)
        stf, mf, stb, mb = _state_call(pm, kt, grow, first_chunks, last_chunks)
        mix = _mixer_call(pm, kt, grow, gcol, stf, mf, stb, mb,
                          mh_norm_w[l].astype(F32).reshape(1, -1), conv_w[l].astype(F32),
                          first_chunks, last_chunks)
        x = _dense_call(x_parts, mix, w_out[l].astype(BF16),
                        ln1_g[l].reshape(1, -1), ln1_b[l].reshape(1, -1),
                        w_ff1[l].astype(BF16), w_ff2[l].astype(BF16),
                        ln2_g[l].reshape(1, -1), ln2_b[l].reshape(1, -1))
        x_parts = (x,)

    n_prompt = x_prompt.shape[0] * x_prompt.shape[1]
    return (x[:n_prompt].reshape(x_prompt.shape), x[n_prompt:].reshape(x_sample.shape))
```

```python
import functools

import jax
import jax.numpy as jnp
from jax import lax
from jax.experimental import pallas as pl
from jax.experimental.pallas import tpu as pltpu

D_MODEL = 1024
DEPTH = 4
N_HEADS = 4
HEAD_DIM = 128
MLSTM_WIDTH = N_HEADS * HEAD_DIM
CONV_WIDTH = D_MODEL - MLSTM_WIDTH
N_DIR = 2
N_HD = N_DIR * N_HEADS
D_FF = 4 * D_MODEL
ALPHA = (2.0 * DEPTH) ** 0.25
LN_EPS = 1e-5
Q_SCALE = HEAD_DIM ** -0.5

LANES = 128
COL_BLOCK = 512
Q_BLK, V_BLK, O_BLK, CB_BLK, U_BLK = range(5)
PM_WIDTH = 5 * COL_BLOCK
MAIN_WIDTH = 6 * COL_BLOCK
GATE_PAD = LANES

CHUNK = 128
ROW_TILE = 512
N_SUB = ROW_TILE // CHUNK
FF_SPLIT = 4
DENSE_TILE = 1024
DENSE_SUB = 512
HALO_ROWS = 16
VMEM_LIMIT = 62 * 1024 * 1024

R_ROW, A_ROW, G_ROW, AMAX_ROW = 0, N_HD, 2 * N_HD, 3 * N_HD
GROW_ROWS = 4 * N_HD
C_TERMS = 2
B_TERMS = 3
GCOL_COLS = (C_TERMS + B_TERMS) * N_HD

F32 = jnp.float32
BF16 = jnp.bfloat16


def _const_spec(shape):
    nd = len(shape)
    return pl.BlockSpec(shape, lambda *_: (0,) * nd, pipeline_mode=pl.Buffered(1))


def _layer_norm(x, g, b):
    mu = jnp.mean(x, axis=-1, keepdims=True)
    xc = x - mu
    var = jnp.mean(xc * xc, axis=-1, keepdims=True)
    return xc * lax.rsqrt(var + LN_EPS) * g + b


def _log_sigmoid(x):
    return jnp.minimum(x, 0.0) - jnp.log1p(jnp.exp(-jnp.abs(x)))


def _is_one_of(idx, values):
    hit = idx == values[0]
    for v in values[1:]:
        hit = jnp.logical_or(hit, idx == v)
    return hit


def _chunk_scan(x, combine, neutral, fwd_row, pos):
    width = x.shape[1]
    s = 1
    while s < CHUNK:
        from_left = pltpu.roll(x, s, axis=1)
        from_right = pltpu.roll(x, width - s, axis=1)
        take_left = jnp.logical_and(fwd_row, pos >= s)
        take_right = jnp.logical_and(jnp.logical_not(fwd_row), pos < CHUNK - s)
        x = combine(x, jnp.where(take_left, from_left, jnp.where(take_right, from_right, neutral)))
        s *= 2
    return x


def _proj_kernel(x_ref, wg_ref, wm_ref, wkt_ref, bias_ref, pm_ref, kt_ref, grow_ref, gcol_ref):
    xb = x_ref[...].astype(BF16)
    g_tok = jnp.dot(xb, wg_ref[...], preferred_element_type=F32)
    kt_ref[...] = lax.dot_general(wkt_ref[...], xb, (((1,), (1,)), ((), ())),
                                  preferred_element_type=F32).astype(BF16)
    p = jnp.dot(xb, wm_ref[...], preferred_element_type=F32)
    pm_ref[:, :COL_BLOCK] = (p[:, :COL_BLOCK] * Q_SCALE).astype(BF16)
    pm_ref[:, COL_BLOCK:U_BLK * COL_BLOCK] = p[:, COL_BLOCK:U_BLK * COL_BLOCK].astype(BF16)
    pm_ref[:, U_BLK * COL_BLOCK:] = (p[:, U_BLK * COL_BLOCK:(U_BLK + 1) * COL_BLOCK]
                                     * p[:, (U_BLK + 1) * COL_BLOCK:]).astype(BF16)

    g_row = g_tok.T[:2 * N_HD, :] + bias_ref[...]
    ig = g_row[:N_HD, :]
    lf = _log_sigmoid(g_row[N_HD:, :])
    fwd_row = lax.broadcasted_iota(jnp.int32, (N_HD, ROW_TILE), 0) < N_HEADS
    pos = jnp.bitwise_and(lax.broadcasted_iota(jnp.int32, (N_HD, ROW_TILE), 1), CHUNK - 1)
    b = _chunk_scan(lf, jnp.add, 0.0, fwd_row, pos)
    r = ig - b
    cm = _chunk_scan(r, jnp.maximum, -jnp.inf, fwd_row, pos)
    at_end = pos == jnp.where(fwd_row, CHUNK - 1, 0)
    b_end = jnp.where(at_end, b, 0.0)
    g_parts, amax_parts = [], []
    for j in range(N_SUB):
        sl = slice(j * CHUNK, (j + 1) * CHUNK)
        g_j = jnp.sum(b_end[:, sl], axis=1, keepdims=True)
        amax_j = g_j + jnp.max(r[:, sl], axis=1, keepdims=True)
        g_parts.append(jnp.broadcast_to(g_j, (N_HD, CHUNK)))
        amax_parts.append(jnp.broadcast_to(amax_j, (N_HD, CHUNK)))
    g_rep = jnp.concatenate(g_parts, axis=1)
    grow_ref[R_ROW:R_ROW + N_HD, :] = r
    grow_ref[A_ROW:A_ROW + N_HD, :] = g_rep + r
    grow_ref[G_ROW:G_ROW + N_HD, :] = g_rep
    grow_ref[AMAX_ROW:AMAX_ROW + N_HD, :] = jnp.concatenate(amax_parts, axis=1)

    terms, rest = [], b
    for _ in range(B_TERMS):
        term = rest.astype(BF16).astype(F32)
        terms.append(term)
        rest = rest - term
    col_src = jnp.concatenate(
        [cm] * C_TERMS + terms + [jnp.zeros((LANES - GCOL_COLS, ROW_TILE), F32)], axis=0)
    gcol_ref[...] = col_src.T[:, :GCOL_COLS]


def _proj_call(x_parts, wg, wm, wkt, bias):
    t = sum(xp.shape[0] for xp in x_parts)
    n_first = x_parts[0].shape[0] // ROW_TILE
    if len(x_parts) == 1:
        x_specs = [pl.BlockSpec((ROW_TILE, D_MODEL), lambda i: (i, 0))]
        body = _proj_kernel
    else:
        x_specs = [pl.BlockSpec((ROW_TILE, D_MODEL), lambda i: (jnp.minimum(i, n_first - 1), 0)),
                   pl.BlockSpec((ROW_TILE, D_MODEL), lambda i: (jnp.maximum(i - n_first, 0), 0))]
        body = functools.partial(_two_source, _proj_kernel, n_first)
    return pl.pallas_call(
        body,
        grid=(t // ROW_TILE,),
        in_specs=x_specs + [
            _const_spec((D_MODEL, GATE_PAD)),
            _const_spec((D_MODEL, MAIN_WIDTH)),
            _const_spec((COL_BLOCK, D_MODEL)),
            _const_spec((2 * N_HD, 1)),
        ],
        out_specs=[
            pl.BlockSpec((ROW_TILE, PM_WIDTH), lambda i: (i, 0)),
            pl.BlockSpec((COL_BLOCK, ROW_TILE), lambda i: (0, i)),
            pl.BlockSpec((GROW_ROWS, ROW_TILE), lambda i: (0, i)),
            pl.BlockSpec((ROW_TILE, GCOL_COLS), lambda i: (i, 0)),
        ],
        out_shape=[
            jax.ShapeDtypeStruct((t, PM_WIDTH), BF16),
            jax.ShapeDtypeStruct((COL_BLOCK, t), BF16),
            jax.ShapeDtypeStruct((GROW_ROWS, t), F32),
            jax.ShapeDtypeStruct((t, GCOL_COLS), F32),
        ],
        compiler_params=pltpu.CompilerParams(
            dimension_semantics=("parallel",), vmem_limit_bytes=VMEM_LIMIT),
        name="proj",
    )(*x_parts, wg, wm, wkt, bias)


class _PickedRef:
    def __init__(self, first, second, use_first):
        self._first, self._second, self._use_first = first, second, use_first

    def __getitem__(self, idx):
        return jnp.where(self._use_first, self._first[idx], self._second[idx])


def _two_source(body, n_first, xa_ref, xb_ref, *rest):
    body(_PickedRef(xa_ref, xb_ref, pl.program_id(0) < n_first), *rest)


def _state_kernel(first_chunks, last_chunks,
                  ktf_ref, vf_ref, gf_ref, ktb_ref, vb_ref, gb_ref,
                  stf_ref, mf_ref, stb_ref, mb_ref,
                  s_scr, m_scr):
    i = pl.program_id(0)
    n_tiles = pl.num_programs(0)

    @pl.when(i == 0)
    def _():
        s_scr[...] = jnp.zeros_like(s_scr)
        m_scr[...] = jnp.zeros_like(m_scr)

    pair_row = lax.broadcasted_iota(jnp.int32, (N_HD, CHUNK), 0)
    fwd_row = pair_row < N_HEADS
    diag = pair_row == jnp.bitwise_and(lax.broadcasted_iota(jnp.int32, (N_HD, CHUNK), 1), N_HD - 1)
    ones = jnp.ones((CHUNK, HEAD_DIM), BF16)
    kt_refs, v_refs, st_refs = (ktf_ref, ktb_ref), (vf_ref, vb_ref), (stf_ref, stb_ref)

    for step in range(N_SUB):
        sub = (step, N_SUB - 1 - step)
        lanes = tuple(slice(s * CHUNK, (s + 1) * CHUNK) for s in sub)
        chunk_f = i * N_SUB + sub[0]
        chunk_b = (n_tiles - 1 - i) * N_SUB + sub[1]
        reset = (_is_one_of(chunk_f, first_chunks), _is_one_of(chunk_b, last_chunks))

        def both(row0):
            return jnp.where(fwd_row, gf_ref[row0:row0 + N_HD, lanes[0]],
                             gb_ref[row0:row0 + N_HD, lanes[1]])

        a, g, amax = both(A_ROW), both(G_ROW), both(AMAX_ROW)
        m_old = m_scr[...]
        m_prev = jnp.where(fwd_row, jnp.where(reset[0], 0.0, m_old), jnp.where(reset[1], 0.0, m_old))
        m_new = jnp.maximum(g + m_prev, amax)
        wa = jnp.exp(a - m_new)
        wc = jnp.exp(g + m_prev - m_new)
        m_scr[...] = m_new
        m_lane = jnp.sum(jnp.where(diag, m_prev, 0.0), axis=0, keepdims=True)
        m_both = jnp.concatenate([m_prev, jnp.broadcast_to(m_lane, (N_HD, CHUNK))], axis=0)
        mf_ref[sub[0]] = m_both
        mb_ref[sub[1]] = m_both

        for d in range(N_DIR):
            for h in range(N_HEADS):
                j = d * N_HEADS + h
                hs = slice(h * HEAD_DIM, (h + 1) * HEAD_DIM)
                rows = slice(sub[d] * CHUNK, (sub[d] + 1) * CHUNK)
                s_prev = jnp.where(reset[d], 0.0, s_scr[j])
                st_refs[d][sub[d], h] = s_prev.astype(BF16)
                kt_w = (kt_refs[d][hs, lanes[d]].astype(F32) * wa[j:j + 1, :]).astype(BF16)
                v_aug = jnp.concatenate([v_refs[d][rows, hs], ones], axis=1)
                upd = jnp.dot(kt_w, v_aug, preferred_element_type=F32)
                wc_j = wc[j:j + 1, :]
                s_scr[j] = jnp.concatenate([wc_j, wc_j], axis=1) * s_prev + upd


def _state_call(pm, kt, grow, first_chunks, last_chunks):
    t = pm.shape[0]
    n = t // ROW_TILE
    n_chunks = t // CHUNK
    rev = lambda i: n - 1 - i
    st_block = (N_SUB, N_HEADS, HEAD_DIM, 2 * HEAD_DIM)
    m_block = (N_SUB, 2 * N_HD, CHUNK)
    st_shape = jax.ShapeDtypeStruct((n_chunks, N_HEADS, HEAD_DIM, 2 * HEAD_DIM), BF16)
    m_shape = jax.ShapeDtypeStruct((n_chunks, 2 * N_HD, CHUNK), F32)
    return pl.pallas_call(
        functools.partial(_state_kernel, first_chunks, last_chunks),
        grid=(n,),
        in_specs=[
            pl.BlockSpec((COL_BLOCK, ROW_TILE), lambda i: (0, i)),
            pl.BlockSpec((ROW_TILE, COL_BLOCK), lambda i: (i, V_BLK)),
            pl.BlockSpec((GROW_ROWS, ROW_TILE), lambda i: (0, i)),
            pl.BlockSpec((COL_BLOCK, ROW_TILE), lambda i: (0, rev(i))),
            pl.BlockSpec((ROW_TILE, COL_BLOCK), lambda i: (rev(i), V_BLK)),
            pl.BlockSpec((GROW_ROWS, ROW_TILE), lambda i: (0, rev(i))),
        ],
        out_specs=[
            pl.BlockSpec(st_block, lambda i: (i, 0, 0, 0)),
            pl.BlockSpec(m_block, lambda i: (i, 0, 0)),
            pl.BlockSpec(st_block, lambda i: (rev(i), 0, 0, 0)),
            pl.BlockSpec(m_block, lambda i: (rev(i), 0, 0)),
        ],
        out_shape=[st_shape, m_shape, st_shape, m_shape],
        scratch_shapes=[
            pltpu.VMEM((N_HD, HEAD_DIM, 2 * HEAD_DIM), F32),
            pltpu.VMEM((N_HD, CHUNK), F32),
        ],
        compiler_params=pltpu.CompilerParams(
            dimension_semantics=("arbitrary",), vmem_limit_bytes=VMEM_LIMIT),
        name="state",
    )(kt, pm, grow, kt, pm, grow)


def _mixer_kernel(q_ref, v_ref, o_ref, kt_ref,
                  r_ref, gcol_ref, stf_ref, mf_ref, stb_ref, mb_ref, spread_ref, nw_ref, out_ref):
    row = lax.broadcasted_iota(jnp.int32, (CHUNK, CHUNK), 0)
    col = lax.broadcasted_iota(jnp.int32, (CHUNK, CHUNK), 1)
    masks = (col <= row, col >= row)
    st_refs = (stf_ref, stb_ref)
    m_refs = (mf_ref, mb_ref)
    ones = jnp.ones((CHUNK, HEAD_DIM), BF16)
    gate_lane = lax.broadcasted_iota(jnp.int32, (CHUNK, GCOL_COLS), 1)
    pair_lane = lax.broadcasted_iota(jnp.int32, (1, GCOL_COLS), 1)

    for sub in range(N_SUB):
        rows = slice(sub * CHUNK, (sub + 1) * CHUNK)
        m_lane = jnp.where(jnp.bitwise_and(pair_lane, N_HD - 1) < N_HEADS,
                           mf_ref[sub, N_HD:N_HD + 1, :GCOL_COLS],
                           mb_ref[sub, N_HD:N_HD + 1, :GCOL_COLS])
        gates = gcol_ref[rows, :]
        c_t = jnp.maximum(gates, m_lane)
        c_hi = c_t.astype(BF16).astype(F32)
        gates = jnp.where(gate_lane < N_HD, c_hi,
                          jnp.where(gate_lane < C_TERMS * N_HD, c_t - c_hi, gates)).astype(BF16)
        for h in range(N_HEADS):
            hs = slice(h * HEAD_DIM, (h + 1) * HEAD_DIM)
            qh = q_ref[rows, hs]
            qh32 = qh.astype(F32)
            s_qk = jnp.dot(qh, kt_ref[hs, rows], preferred_element_type=F32)
            v_aug = jnp.concatenate([v_ref[rows, hs], ones], axis=1)
            h_sum = None
            for d in range(N_DIR):
                j = d * N_HEADS + h
                spread = jnp.dot(gates, spread_ref[j], preferred_element_type=F32)
                c_b = spread[:, :CHUNK]
                m_prev = m_refs[d][sub, j:j + 1, :]
                w_d = jnp.exp(jnp.where(masks[d], r_ref[j:j + 1, rows] - c_b, -jnp.inf))
                s_w = (s_qk * w_d).astype(BF16)
                q_w = (qh32 * jnp.exp(m_prev - c_b)).astype(BF16)
                tot = jnp.dot(jnp.concatenate([s_w, q_w], axis=1),
                              jnp.concatenate([v_aug, st_refs[d][sub, h]], axis=0),
                              preferred_element_type=F32)
                floor = jnp.exp(-spread[:, CHUNK:])
                h_dir = tot[:, :HEAD_DIM] / jnp.maximum(jnp.abs(tot[:, HEAD_DIM:]), floor)
                h_sum = h_dir if h_sum is None else h_sum + h_dir
            mu = jnp.mean(h_sum, axis=-1, keepdims=True)
            hc = h_sum - mu
            hn = hc * lax.rsqrt(jnp.mean(hc * hc, axis=-1, keepdims=True) + LN_EPS)
            hn = hn * nw_ref[:, hs]
            out_ref[rows, hs] = (jax.nn.sigmoid(o_ref[rows, hs].astype(F32)) * hn).astype(BF16)


def _mixer_call(pm, kt, grow, gcol, stf, mf, stb, mb, norm_w):
    t = pm.shape[0]
    n = t // ROW_TILE
    blk = lambda col: pl.BlockSpec((ROW_TILE, COL_BLOCK), lambda i: (i, col))
    st_spec = pl.BlockSpec((N_SUB, N_HEADS, HEAD_DIM, 2 * HEAD_DIM), lambda i: (i, 0, 0, 0))
    m_spec = pl.BlockSpec((N_SUB, 2 * N_HD, CHUNK), lambda i: (i, 0, 0))
    return pl.pallas_call(
        _mixer_kernel,
        grid=(n,),
        in_specs=[blk(Q_BLK), blk(V_BLK), blk(O_BLK),
                  pl.BlockSpec((COL_BLOCK, ROW_TILE), lambda i: (0, i)),
                  pl.BlockSpec((N_HD, ROW_TILE), lambda i: (0, i)),
                  pl.BlockSpec((ROW_TILE, GCOL_COLS), lambda i: (i, 0)),
                  st_spec, m_spec, st_spec, m_spec,
                  _const_spec((N_HD, GCOL_COLS, 2 * CHUNK)),
                  _const_spec((1, MLSTM_WIDTH))],
        out_specs=pl.BlockSpec((ROW_TILE, MLSTM_WIDTH), lambda i: (i, 0)),
        out_shape=jax.ShapeDtypeStruct((t, MLSTM_WIDTH), BF16),
        compiler_params=pltpu.CompilerParams(
            dimension_semantics=("parallel",), vmem_limit_bytes=VMEM_LIMIT),
        name="mixer",
    )(pm, pm, pm, kt, grow, gcol, stf, mf, stb, mb, _spread_matrix(), norm_w)


def _spread_matrix():
    term = jnp.arange(GCOL_COLS)[None, :, None] // N_HD
    pair = jnp.arange(GCOL_COLS)[None, :, None] % N_HD
    want = jnp.arange(N_HD)[:, None, None]
    lane = jnp.arange(2 * CHUNK)[None, None, :]
    hit = (pair == want) & ((term < C_TERMS) | (lane >= CHUNK))
    return hit.astype(BF16)


def _dense_kernel(first_subs, last_subs,
                  x_ref, hm_ref, cb_ref, u_ref, up_ref, un_ref, cw_ref, wo_ref,
                  g1_ref, b1_ref, w1_ref, w2_ref, g2_ref, b2_ref, out_ref):
    slab = D_FF // FF_SPLIT
    n_sub = DENSE_TILE // DENSE_SUB
    rows = [slice(r * DENSE_SUB, (r + 1) * DENSE_SUB) for r in range(n_sub)]
    tok = lax.broadcasted_iota(jnp.int32, (DENSE_SUB, CONV_WIDTH), 0)

    def conv_gate(r):
        group = pl.program_id(0) * n_sub + r
        u = u_ref[rows[r], :].astype(F32)
        before = (up_ref[HALO_ROWS - 1:HALO_ROWS, :] if r == 0
                  else u_ref[r * DENSE_SUB - 1:r * DENSE_SUB, :]).astype(F32)
        after = (un_ref[0:1, :] if r == n_sub - 1
                 else u_ref[(r + 1) * DENSE_SUB:(r + 1) * DENSE_SUB + 1, :]).astype(F32)
        before = jnp.where(_is_one_of(group, first_subs), 0.0, before)
        after = jnp.where(_is_one_of(group, last_subs), 0.0, after)
        u_prev = jnp.where(tok == 0, before, pltpu.roll(u, 1, axis=0))
        u_next = jnp.where(tok == DENSE_SUB - 1, after, pltpu.roll(u, DENSE_SUB - 1, axis=0))
        y = cw_ref[0:1, :] * u_prev + cw_ref[1:2, :] * u + cw_ref[2:3, :] * u_next
        return (cb_ref[rows[r], :].astype(F32) * y).astype(BF16)

    def mix_out(r):
        y = jnp.dot(hm_ref[rows[r], :], wo_ref[:MLSTM_WIDTH, :], preferred_element_type=F32)
        return y + jnp.dot(conv_gate(r), wo_ref[MLSTM_WIDTH:, :], preferred_element_type=F32)

    def norm1(r, y):
        x1 = _layer_norm(ALPHA * x_ref[rows[r], :] + y, g1_ref[...], b1_ref[...])
        return x1.astype(BF16), ALPHA * x1

    def ffn_slab(j, x1b, acc):
        hid = jnp.dot(x1b, w1_ref[:, j * slab:(j + 1) * slab], preferred_element_type=F32)
        hid = jnp.square(jnp.maximum(hid, 0.0)).astype(BF16)
        return acc + jnp.dot(hid, w2_ref[j * slab:(j + 1) * slab, :], preferred_element_type=F32)

    state = [None] * n_sub
    state[0] = norm1(0, mix_out(0))
    for r in range(n_sub):
        x1b, acc = state[r]
        acc = ffn_slab(0, x1b, acc)
        if r + 1 < n_sub:
            state[r + 1] = norm1(r + 1, mix_out(r + 1))
        if r > 0:
            out_ref[rows[r - 1], :] = _layer_norm(state[r - 1], g2_ref[...], b2_ref[...])
        for j in range(1, FF_SPLIT):
            acc = ffn_slab(j, x1b, acc)
        state[r] = acc
    out_ref[rows[n_sub - 1], :] = _layer_norm(state[n_sub - 1], g2_ref[...], b2_ref[...])


def _dense_call(x_parts, hm, pm, conv_w, wo, g1, b1, w1, w2, g2, b2, first_subs, last_subs):
    t = hm.shape[0]
    n_first = x_parts[0].shape[0] // DENSE_TILE
    per_halo = DENSE_TILE // HALO_ROWS
    n_halo = t // HALO_ROWS
    row = lambda: pl.BlockSpec((DENSE_TILE, D_MODEL), lambda i: (i, 0))
    blk = lambda col: pl.BlockSpec((DENSE_TILE, COL_BLOCK), lambda i: (i, col))
    vec = lambda: _const_spec((1, D_MODEL))
    body = functools.partial(_dense_kernel, first_subs, last_subs)
    if len(x_parts) == 1:
        x_specs = [row()]
    else:
        x_specs = [pl.BlockSpec((DENSE_TILE, D_MODEL), lambda i: (jnp.minimum(i, n_first - 1), 0)),
                   pl.BlockSpec((DENSE_TILE, D_MODEL), lambda i: (jnp.maximum(i - n_first, 0), 0))]
        body = functools.partial(_two_source, body, n_first)
    return pl.pallas_call(
        body,
        grid=(t // DENSE_TILE,),
        in_specs=x_specs + [
            pl.BlockSpec((DENSE_TILE, MLSTM_WIDTH), lambda i: (i, 0)),
            blk(CB_BLK), blk(U_BLK),
            pl.BlockSpec((HALO_ROWS, COL_BLOCK), lambda i: (jnp.maximum(i * per_halo - 1, 0), U_BLK)),
            pl.BlockSpec((HALO_ROWS, COL_BLOCK),
                         lambda i: (jnp.minimum((i + 1) * per_halo, n_halo - 1), U_BLK)),
            _const_spec((3, CONV_WIDTH)),
            _const_spec((D_MODEL, D_MODEL)), vec(), vec(),
            _const_spec((D_MODEL, D_FF)), _const_spec((D_FF, D_MODEL)), vec(), vec()],
        out_specs=row(),
        out_shape=jax.ShapeDtypeStruct((t, D_MODEL), F32),
        compiler_params=pltpu.CompilerParams(
            dimension_semantics=("parallel",), vmem_limit_bytes=VMEM_LIMIT),
        name="dense",
    )(*x_parts, hm, pm, pm, pm, pm, conv_w, wo, g1, b1, w1, w2, g2, b2)


def kernel(x_prompt, x_sample, w_in, b_gate, mh_norm_w, conv_w, w_out, ln1_g, ln1_b, w_ff1, w_ff2, ln2_g, ln2_b):
    seq_lens = [x_prompt.shape[1]] * x_prompt.shape[0] + [x_sample.shape[1]] * x_sample.shape[0]
    assert all(s % ROW_TILE == 0 and s % DENSE_SUB == 0 for s in seq_lens)
    starts = [sum(seq_lens[:i]) for i in range(len(seq_lens))]
    first_chunks = tuple(s // CHUNK for s in starts)
    last_chunks = tuple((s + n) // CHUNK - 1 for s, n in zip(starts, seq_lens))
    first_subs = tuple(s // DENSE_SUB for s in starts)
    last_subs = tuple((s + n) // DENSE_SUB - 1 for s, n in zip(starts, seq_lens))

    x_parts = (x_prompt.reshape(-1, D_MODEL), x_sample.reshape(-1, D_MODEL))

    k_lo, v_lo = MLSTM_WIDTH, 2 * MLSTM_WIDTH
    gate_lo = 4 * MLSTM_WIDTH
    gate_hi = gate_lo + 2 * N_HD
    for l in range(DEPTH):
        w_gate = (w_in[l, :, gate_lo:gate_hi].reshape(D_MODEL, N_DIR, 2, N_HEADS)
                  .transpose(0, 2, 1, 3).reshape(D_MODEL, 2 * N_HD))
        w_gate = jnp.pad(w_gate, ((0, 0), (0, GATE_PAD - 2 * N_HD))).astype(BF16)
        w_main = jnp.concatenate(
            [w_in[l, :, :k_lo], w_in[l, :, v_lo:gate_lo], w_in[l, :, gate_hi:]], axis=1).astype(BF16)
        w_kt = w_in[l, :, k_lo:v_lo].T.astype(BF16)
        bias = b_gate[l].astype(F32).transpose(1, 0, 2).reshape(2 * N_HD, 1)
        pm, kt, grow, gcol = _proj_call(x_parts, w_gate, w_main, w_kt, bias)
        stf, mf, stb, mb = _state_call(pm, kt, grow, first_chunks, last_chunks)
        hm = _mixer_call(pm, kt, grow, gcol, stf, mf, stb, mb, mh_norm_w[l].astype(F32).reshape(1, -1))
        x = _dense_call(x_parts, hm, pm, conv_w[l].astype(F32), w_out[l].astype(BF16),
                        ln1_g[l].reshape(1, -1), ln1_b[l].reshape(1, -1),
                        w_ff1[l].astype(BF16), w_ff2[l].astype(BF16),
                        ln2_g[l].reshape(1, -1), ln2_b[l].reshape(1, -1), first_subs, last_subs)
        x_parts = (x,)

    n_prompt = x_prompt.shape[0] * x_prompt.shape[1]
    return (x[:n_prompt].reshape(x_prompt.shape), x[n_prompt:].reshape(x_sample.shape))
```

```python
import functools

import jax
import jax.numpy as jnp
from jax import lax
from jax.experimental import pallas as pl
from jax.experimental.pallas import tpu as pltpu

D_MODEL = 1024
DEPTH = 4
N_HEADS = 4
HEAD_DIM = 128
MLSTM_WIDTH = N_HEADS * HEAD_DIM
CONV_WIDTH = D_MODEL - MLSTM_WIDTH
N_DIR = 2
N_HD = N_DIR * N_HEADS
D_FF = 4 * D_MODEL
ALPHA = (2.0 * DEPTH) ** 0.25
LN_EPS = 1e-5
Q_SCALE = HEAD_DIM ** -0.5
LOG2E = 1.4426950408889634

LANES = 128
COL_BLOCK = 512
Q_BLK, V_BLK, O_BLK, CB_BLK, U_BLK = range(5)
PM_WIDTH = 5 * COL_BLOCK
MAIN_WIDTH = 6 * COL_BLOCK
GATE_PAD = LANES

CHUNK = 128
ROW_TILE = 512
N_SUB = ROW_TILE // CHUNK
MIX_TILE = 1024
FF_SPLIT = 4
DENSE_TILE = 1024
DENSE_SUB = 512
HALO_ROWS = 16
VMEM_LIMIT = 62 * 1024 * 1024

R_ROW, A_ROW, G_ROW, AMAX_ROW = 0, N_HD, 2 * N_HD, 3 * N_HD
GROW_ROWS = 4 * N_HD
C_TERMS = 2
B_TERMS = 3
GCOL_COLS = (C_TERMS + B_TERMS) * N_HD

F32 = jnp.float32
BF16 = jnp.bfloat16


def _const_spec(shape):
    nd = len(shape)
    return pl.BlockSpec(shape, lambda *_: (0,) * nd, pipeline_mode=pl.Buffered(1))


def _layer_norm(x, g, b):
    mu = jnp.mean(x, axis=-1, keepdims=True)
    xc = x - mu
    var = jnp.mean(xc * xc, axis=-1, keepdims=True)
    return xc * lax.rsqrt(var + LN_EPS) * g + b


def _log_sigmoid(x):
    return jnp.minimum(x, 0.0) - jnp.log1p(jnp.exp(-jnp.abs(x)))


def _is_one_of(idx, values):
    hit = idx == values[0]
    for v in values[1:]:
        hit = jnp.logical_or(hit, idx == v)
    return hit


def _chunk_scan(x, combine, neutral, fwd_row, pos):
    width = x.shape[1]
    s = 1
    while s < CHUNK:
        from_left = pltpu.roll(x, s, axis=1)
        from_right = pltpu.roll(x, width - s, axis=1)
        take_left = jnp.logical_and(fwd_row, pos >= s)
        take_right = jnp.logical_and(jnp.logical_not(fwd_row), pos < CHUNK - s)
        x = combine(x, jnp.where(take_left, from_left, jnp.where(take_right, from_right, neutral)))
        s *= 2
    return x


def _proj_kernel(x_ref, wg_ref, wm_ref, wkt_ref, bias_ref, pm_ref, kt_ref, grow_ref, gcol_ref):
    xb = x_ref[...].astype(BF16)
    g_tok = jnp.dot(xb, wg_ref[...], preferred_element_type=F32)
    kt_ref[...] = lax.dot_general(wkt_ref[...], xb, (((1,), (1,)), ((), ())),
                                  preferred_element_type=F32).astype(BF16)
    p = jnp.dot(xb, wm_ref[...], preferred_element_type=F32)
    pm_ref[:, :COL_BLOCK] = (p[:, :COL_BLOCK] * Q_SCALE).astype(BF16)
    pm_ref[:, COL_BLOCK:U_BLK * COL_BLOCK] = p[:, COL_BLOCK:U_BLK * COL_BLOCK].astype(BF16)
    pm_ref[:, U_BLK * COL_BLOCK:] = (p[:, U_BLK * COL_BLOCK:(U_BLK + 1) * COL_BLOCK]
                                     * p[:, (U_BLK + 1) * COL_BLOCK:]).astype(BF16)

    g_row = g_tok.T[:2 * N_HD, :] + bias_ref[...]
    ig = g_row[:N_HD, :]
    lf = _log_sigmoid(g_row[N_HD:, :])
    fwd_row = lax.broadcasted_iota(jnp.int32, (N_HD, ROW_TILE), 0) < N_HEADS
    pos = jnp.bitwise_and(lax.broadcasted_iota(jnp.int32, (N_HD, ROW_TILE), 1), CHUNK - 1)
    b = _chunk_scan(lf, jnp.add, 0.0, fwd_row, pos)
    r = ig - b
    cm = _chunk_scan(r, jnp.maximum, -jnp.inf, fwd_row, pos)
    at_end = pos == jnp.where(fwd_row, CHUNK - 1, 0)
    b_end = jnp.where(at_end, b, 0.0)
    g_parts, amax_parts = [], []
    for j in range(N_SUB):
        sl = slice(j * CHUNK, (j + 1) * CHUNK)
        g_j = jnp.sum(b_end[:, sl], axis=1, keepdims=True)
        amax_j = g_j + jnp.max(r[:, sl], axis=1, keepdims=True)
        g_parts.append(jnp.broadcast_to(g_j, (N_HD, CHUNK)))
        amax_parts.append(jnp.broadcast_to(amax_j, (N_HD, CHUNK)))
    g_rep = jnp.concatenate(g_parts, axis=1)
    grow_ref[R_ROW:R_ROW + N_HD, :] = r * LOG2E
    grow_ref[A_ROW:A_ROW + N_HD, :] = g_rep + r
    grow_ref[G_ROW:G_ROW + N_HD, :] = g_rep
    grow_ref[AMAX_ROW:AMAX_ROW + N_HD, :] = jnp.concatenate(amax_parts, axis=1)

    terms, rest = [], b * LOG2E
    for _ in range(B_TERMS):
        term = rest.astype(BF16).astype(F32)
        terms.append(term)
        rest = rest - term
    col_src = jnp.concatenate(
        [cm * LOG2E] * C_TERMS + terms + [jnp.zeros((LANES - GCOL_COLS, ROW_TILE), F32)], axis=0)
    gcol_ref[...] = col_src.T[:, :GCOL_COLS]


def _proj_call(x_parts, wg, wm, wkt, bias):
    t = sum(xp.shape[0] for xp in x_parts)
    n_first = x_parts[0].shape[0] // ROW_TILE
    if len(x_parts) == 1:
        x_specs = [pl.BlockSpec((ROW_TILE, D_MODEL), lambda i: (i, 0))]
        body = _proj_kernel
    else:
        x_specs = [pl.BlockSpec((ROW_TILE, D_MODEL), lambda i: (jnp.minimum(i, n_first - 1), 0)),
                   pl.BlockSpec((ROW_TILE, D_MODEL), lambda i: (jnp.maximum(i - n_first, 0), 0))]
        body = functools.partial(_two_source, _proj_kernel, n_first)
    return pl.pallas_call(
        body,
        grid=(t // ROW_TILE,),
        in_specs=x_specs + [
            _const_spec((D_MODEL, GATE_PAD)),
            _const_spec((D_MODEL, MAIN_WIDTH)),
            _const_spec((COL_BLOCK, D_MODEL)),
            _const_spec((2 * N_HD, 1)),
        ],
        out_specs=[
            pl.BlockSpec((ROW_TILE, PM_WIDTH), lambda i: (i, 0)),
            pl.BlockSpec((COL_BLOCK, ROW_TILE), lambda i: (0, i)),
            pl.BlockSpec((GROW_ROWS, ROW_TILE), lambda i: (0, i)),
            pl.BlockSpec((ROW_TILE, GCOL_COLS), lambda i: (i, 0)),
        ],
        out_shape=[
            jax.ShapeDtypeStruct((t, PM_WIDTH), BF16),
            jax.ShapeDtypeStruct((COL_BLOCK, t), BF16),
            jax.ShapeDtypeStruct((GROW_ROWS, t), F32),
            jax.ShapeDtypeStruct((t, GCOL_COLS), F32),
        ],
        compiler_params=pltpu.CompilerParams(
            dimension_semantics=("parallel",), vmem_limit_bytes=VMEM_LIMIT),
        name="proj",
    )(*x_parts, wg, wm, wkt, bias)


class _PickedRef:
    def __init__(self, first, second, use_first):
        self._first, self._second, self._use_first = first, second, use_first

    def __getitem__(self, idx):
        return jnp.where(self._use_first, self._first[idx], self._second[idx])


def _two_source(body, n_first, xa_ref, xb_ref, *rest):
    body(_PickedRef(xa_ref, xb_ref, pl.program_id(0) < n_first), *rest)


def _state_kernel(first_chunks, last_chunks,
                  ktf_ref, vf_ref, gf_ref, ktb_ref, vb_ref, gb_ref,
                  stf_ref, mf_ref, stb_ref, mb_ref,
                  s_scr, m_scr):
    i = pl.program_id(0)
    n_tiles = pl.num_programs(0)

    @pl.when(i == 0)
    def _():
        s_scr[...] = jnp.zeros_like(s_scr)
        m_scr[...] = jnp.zeros_like(m_scr)

    pair_row = lax.broadcasted_iota(jnp.int32, (N_HD, CHUNK), 0)
    fwd_row = pair_row < N_HEADS
    diag = pair_row == jnp.bitwise_and(lax.broadcasted_iota(jnp.int32, (N_HD, CHUNK), 1), N_HD - 1)
    ones = jnp.ones((CHUNK, HEAD_DIM), BF16)
    kt_refs, v_refs, st_refs = (ktf_ref, ktb_ref), (vf_ref, vb_ref), (stf_ref, stb_ref)

    for step in range(N_SUB):
        sub = (step, N_SUB - 1 - step)
        lanes = tuple(slice(s * CHUNK, (s + 1) * CHUNK) for s in sub)
        chunk_f = i * N_SUB + sub[0]
        chunk_b = (n_tiles - 1 - i) * N_SUB + sub[1]
        reset = (_is_one_of(chunk_f, first_chunks), _is_one_of(chunk_b, last_chunks))

        def both(row0):
            return jnp.where(fwd_row, gf_ref[row0:row0 + N_HD, lanes[0]],
                             gb_ref[row0:row0 + N_HD, lanes[1]])

        a, g, amax = both(A_ROW), both(G_ROW), both(AMAX_ROW)
        m_old = m_scr[...]
        m_prev = jnp.where(fwd_row, jnp.where(reset[0], 0.0, m_old), jnp.where(reset[1], 0.0, m_old))
        m_new = jnp.maximum(g + m_prev, amax)
        wa = jnp.exp(a - m_new)
        wc = jnp.exp(g + m_prev - m_new)
        m_scr[...] = m_new
        m_lane = jnp.sum(jnp.where(diag, m_prev, 0.0), axis=0, keepdims=True)
        m_both = jnp.concatenate([m_prev, jnp.broadcast_to(m_lane, (N_HD, CHUNK))], axis=0)
        mf_ref[sub[0]] = m_both
        mb_ref[sub[1]] = m_both

        for d in range(N_DIR):
            for h in range(N_HEADS):
                j = d * N_HEADS + h
                hs = slice(h * HEAD_DIM, (h + 1) * HEAD_DIM)
                rows = slice(sub[d] * CHUNK, (sub[d] + 1) * CHUNK)
                s_prev = jnp.where(reset[d], 0.0, s_scr[j])
                st_refs[d][sub[d], h] = s_prev.astype(BF16)
                kt_w = (kt_refs[d][hs, lanes[d]].astype(F32) * wa[j:j + 1, :]).astype(BF16)
                v_aug = jnp.concatenate([v_refs[d][rows, hs], ones], axis=1)
                upd = jnp.dot(kt_w, v_aug, preferred_element_type=F32)
                wc_j = wc[j:j + 1, :]
                s_scr[j] = jnp.concatenate([wc_j, wc_j], axis=1) * s_prev + upd


def _state_call(pm, kt, grow, first_chunks, last_chunks):
    t = pm.shape[0]
    n = t // ROW_TILE
    n_chunks = t // CHUNK
    rev = lambda i: n - 1 - i
    st_block = (N_SUB, N_HEADS, HEAD_DIM, 2 * HEAD_DIM)
    m_block = (N_SUB, 2 * N_HD, CHUNK)
    st_shape = jax.ShapeDtypeStruct((n_chunks, N_HEADS, HEAD_DIM, 2 * HEAD_DIM), BF16)
    m_shape = jax.ShapeDtypeStruct((n_chunks, 2 * N_HD, CHUNK), F32)
    return pl.pallas_call(
        functools.partial(_state_kernel, first_chunks, last_chunks),
        grid=(n,),
        in_specs=[
            pl.BlockSpec((COL_BLOCK, ROW_TILE), lambda i: (0, i)),
            pl.BlockSpec((ROW_TILE, COL_BLOCK), lambda i: (i, V_BLK)),
            pl.BlockSpec((GROW_ROWS, ROW_TILE), lambda i: (0, i)),
            pl.BlockSpec((COL_BLOCK, ROW_TILE), lambda i: (0, rev(i))),
            pl.BlockSpec((ROW_TILE, COL_BLOCK), lambda i: (rev(i), V_BLK)),
            pl.BlockSpec((GROW_ROWS, ROW_TILE), lambda i: (0, rev(i))),
        ],
        out_specs=[
            pl.BlockSpec(st_block, lambda i: (i, 0, 0, 0)),
            pl.BlockSpec(m_block, lambda i: (i, 0, 0)),
            pl.BlockSpec(st_block, lambda i: (rev(i), 0, 0, 0)),
            pl.BlockSpec(m_block, lambda i: (rev(i), 0, 0)),
        ],
        out_shape=[st_shape, m_shape, st_shape, m_shape],
        scratch_shapes=[
            pltpu.VMEM((N_HD, HEAD_DIM, 2 * HEAD_DIM), F32),
            pltpu.VMEM((N_HD, CHUNK), F32),
        ],
        compiler_params=pltpu.CompilerParams(
            dimension_semantics=("arbitrary",), vmem_limit_bytes=VMEM_LIMIT),
        name="state",
    )(kt, pm, grow, kt, pm, grow)


def _mixer_kernel(q_ref, v_ref, o_ref, kt_ref,
                  r_ref, gcol_ref, stf_ref, mf_ref, stb_ref, mb_ref, spread_ref, nw_ref, out_ref):
    row = lax.broadcasted_iota(jnp.int32, (CHUNK, CHUNK), 0)
    col = lax.broadcasted_iota(jnp.int32, (CHUNK, CHUNK), 1)
    masks = (col <= row, col >= row)
    st_refs = (stf_ref, stb_ref)
    m_refs = (mf_ref, mb_ref)
    ones = jnp.ones((CHUNK, HEAD_DIM), BF16)
    gate_lane = lax.broadcasted_iota(jnp.int32, (CHUNK, GCOL_COLS), 1)
    pair_lane = lax.broadcasted_iota(jnp.int32, (1, GCOL_COLS), 1)

    for sub in range(MIX_TILE // CHUNK):
        rows = slice(sub * CHUNK, (sub + 1) * CHUNK)
        m_lane = LOG2E * jnp.where(jnp.bitwise_and(pair_lane, N_HD - 1) < N_HEADS,
                                   mf_ref[sub, N_HD:N_HD + 1, :GCOL_COLS],
                                   mb_ref[sub, N_HD:N_HD + 1, :GCOL_COLS])
        gates = gcol_ref[rows, :]
        c_t = jnp.maximum(gates, m_lane)
        c_hi = c_t.astype(BF16).astype(F32)
        gates = jnp.where(gate_lane < N_HD, c_hi,
                          jnp.where(gate_lane < C_TERMS * N_HD, c_t - c_hi, gates)).astype(BF16)
        for h in range(N_HEADS):
            hs = slice(h * HEAD_DIM, (h + 1) * HEAD_DIM)
            qh = q_ref[rows, hs]
            qh32 = qh.astype(F32)
            s_qk = jnp.dot(qh, kt_ref[hs, rows], preferred_element_type=F32)
            v_aug = jnp.concatenate([v_ref[rows, hs], ones], axis=1)
            h_sum = None
            for d in range(N_DIR):
                j = d * N_HEADS + h
                spread = jnp.dot(gates, spread_ref[j], preferred_element_type=F32)
                c_b = spread[:, :CHUNK]
                m_prev = LOG2E * m_refs[d][sub, j:j + 1, :]
                w_d = jnp.exp2(jnp.where(masks[d], r_ref[j:j + 1, rows] - c_b, -jnp.inf))
                s_w = (s_qk * w_d).astype(BF16)
                q_w = (qh32 * jnp.exp2(m_prev - c_b)).astype(BF16)
                tot = jnp.dot(jnp.concatenate([s_w, q_w], axis=1),
                              jnp.concatenate([v_aug, st_refs[d][sub, h]], axis=0),
                              preferred_element_type=F32)
                floor = jnp.exp2(-spread[:, CHUNK:])
                h_dir = tot[:, :HEAD_DIM] / jnp.maximum(jnp.abs(tot[:, HEAD_DIM:]), floor)
                h_sum = h_dir if h_sum is None else h_sum + h_dir
            mu = jnp.mean(h_sum, axis=-1, keepdims=True)
            hc = h_sum - mu
            hn = hc * lax.rsqrt(jnp.mean(hc * hc, axis=-1, keepdims=True) + LN_EPS)
            hn = hn * nw_ref[:, hs]
            out_ref[rows, hs] = (jax.nn.sigmoid(o_ref[rows, hs].astype(F32)) * hn).astype(BF16)


def _mixer_call(pm, kt, grow, gcol, stf, mf, stb, mb, norm_w):
    t = pm.shape[0]
    n = t // MIX_TILE
    blk = lambda col: pl.BlockSpec((MIX_TILE, COL_BLOCK), lambda i: (i, col))
    st_spec = pl.BlockSpec((MIX_TILE // CHUNK, N_HEADS, HEAD_DIM, 2 * HEAD_DIM), lambda i: (i, 0, 0, 0))
    m_spec = pl.BlockSpec((MIX_TILE // CHUNK, 2 * N_HD, CHUNK), lambda i: (i, 0, 0))
    return pl.pallas_call(
        _mixer_kernel,
        grid=(n,),
        in_specs=[blk(Q_BLK), blk(V_BLK), blk(O_BLK),
                  pl.BlockSpec((COL_BLOCK, MIX_TILE), lambda i: (0, i)),
                  pl.BlockSpec((N_HD, MIX_TILE), lambda i: (0, i)),
                  pl.BlockSpec((MIX_TILE, GCOL_COLS), lambda i: (i, 0)),
                  st_spec, m_spec, st_spec, m_spec,
                  _const_spec((N_HD, GCOL_COLS, 2 * CHUNK)),
                  _const_spec((1, MLSTM_WIDTH))],
        out_specs=pl.BlockSpec((MIX_TILE, MLSTM_WIDTH), lambda i: (i, 0)),
        out_shape=jax.ShapeDtypeStruct((t, MLSTM_WIDTH), BF16),
        compiler_params=pltpu.CompilerParams(
            dimension_semantics=("parallel",), vmem_limit_bytes=VMEM_LIMIT),
        name="mixer",
    )(pm, pm, pm, kt, grow, gcol, stf, mf, stb, mb, _spread_matrix(), norm_w)


def _spread_matrix():
    term = jnp.arange(GCOL_COLS)[None, :, None] // N_HD
    pair = jnp.arange(GCOL_COLS)[None, :, None] % N_HD
    want = jnp.arange(N_HD)[:, None, None]
    lane = jnp.arange(2 * CHUNK)[None, None, :]
    hit = (pair == want) & ((term < C_TERMS) | (lane >= CHUNK))
    return hit.astype(BF16)


def _dense_kernel(tile0, first_subs, last_subs,
                  x_ref, hm_ref, cb_ref, u_ref, up_ref, un_ref, cw_ref, wo_ref,
                  g1_ref, b1_ref, w1_ref, w2_ref, g2_ref, b2_ref, out_ref):
    slab = D_FF // FF_SPLIT
    n_sub = DENSE_TILE // DENSE_SUB
    rows = [slice(r * DENSE_SUB, (r + 1) * DENSE_SUB) for r in range(n_sub)]
    tok = lax.broadcasted_iota(jnp.int32, (DENSE_SUB, CONV_WIDTH), 0)

    def conv_gate(r):
        group = (pl.program_id(0) + tile0) * n_sub + r
        u = u_ref[rows[r], :].astype(F32)
        before = (up_ref[HALO_ROWS - 1:HALO_ROWS, :] if r == 0
                  else u_ref[r * DENSE_SUB - 1:r * DENSE_SUB, :]).astype(F32)
        after = (un_ref[0:1, :] if r == n_sub - 1
                 else u_ref[(r + 1) * DENSE_SUB:(r + 1) * DENSE_SUB + 1, :]).astype(F32)
        before = jnp.where(_is_one_of(group, first_subs), 0.0, before)
        after = jnp.where(_is_one_of(group, last_subs), 0.0, after)
        u_prev = jnp.where(tok == 0, before, pltpu.roll(u, 1, axis=0))
        u_next = jnp.where(tok == DENSE_SUB - 1, after, pltpu.roll(u, DENSE_SUB - 1, axis=0))
        y = cw_ref[0:1, :] * u_prev + cw_ref[1:2, :] * u + cw_ref[2:3, :] * u_next
        return (cb_ref[rows[r], :].astype(F32) * y).astype(BF16)

    def mix_out(r):
        y = jnp.dot(hm_ref[rows[r], :], wo_ref[:MLSTM_WIDTH, :], preferred_element_type=F32)
        return y + jnp.dot(conv_gate(r), wo_ref[MLSTM_WIDTH:, :], preferred_element_type=F32)

    def norm1(r, y):
        x1 = _layer_norm(ALPHA * x_ref[rows[r], :] + y, g1_ref[...], b1_ref[...])
        return x1.astype(BF16), ALPHA * x1

    def ffn_slab(j, x1b, acc):
        hid = jnp.dot(x1b, w1_ref[:, j * slab:(j + 1) * slab], preferred_element_type=F32)
        hid = jnp.square(jnp.maximum(hid, 0.0)).astype(BF16)
        return acc + jnp.dot(hid, w2_ref[j * slab:(j + 1) * slab, :], preferred_element_type=F32)

    state = [None] * n_sub
    state[0] = norm1(0, mix_out(0))
    for r in range(n_sub):
        x1b, acc = state[r]
        acc = ffn_slab(0, x1b, acc)
        if r + 1 < n_sub:
            state[r + 1] = norm1(r + 1, mix_out(r + 1))
        if r > 0:
            out_ref[rows[r - 1], :] = _layer_norm(state[r - 1], g2_ref[...], b2_ref[...])
        for j in range(1, FF_SPLIT):
            acc = ffn_slab(j, x1b, acc)
        state[r] = acc
    out_ref[rows[n_sub - 1], :] = _layer_norm(state[n_sub - 1], g2_ref[...], b2_ref[...])


def _dense_call(x_parts, hm, pm, conv_w, wo, g1, b1, w1, w2, g2, b2, first_subs, last_subs,
                tile_range=None):
    t = hm.shape[0]
    tile0, n_tiles = tile_range if tile_range is not None else (0, t // DENSE_TILE)
    n_first = x_parts[0].shape[0] // DENSE_TILE
    per_halo = DENSE_TILE // HALO_ROWS
    n_halo = t // HALO_ROWS
    row = lambda: pl.BlockSpec((DENSE_TILE, D_MODEL), lambda i: (i + tile0, 0))
    blk = lambda col: pl.BlockSpec((DENSE_TILE, COL_BLOCK), lambda i: (i + tile0, col))
    vec = lambda: _const_spec((1, D_MODEL))
    body = functools.partial(_dense_kernel, tile0, first_subs, last_subs)
    if len(x_parts) == 1:
        x_specs = [row()]
    else:
        assert tile0 == 0
        x_specs = [pl.BlockSpec((DENSE_TILE, D_MODEL), lambda i: (jnp.minimum(i, n_first - 1), 0)),
                   pl.BlockSpec((DENSE_TILE, D_MODEL), lambda i: (jnp.maximum(i - n_first, 0), 0))]
        body = functools.partial(_two_source, body, n_first)
    return pl.pallas_call(
        body,
        grid=(n_tiles,),
        in_specs=x_specs + [
            pl.BlockSpec((DENSE_TILE, MLSTM_WIDTH), lambda i: (i + tile0, 0)),
            blk(CB_BLK), blk(U_BLK),
            pl.BlockSpec((HALO_ROWS, COL_BLOCK),
                         lambda i: (jnp.maximum((i + tile0) * per_halo - 1, 0), U_BLK)),
            pl.BlockSpec((HALO_ROWS, COL_BLOCK),
                         lambda i: (jnp.minimum((i + tile0 + 1) * per_halo, n_halo - 1), U_BLK)),
            _const_spec((3, CONV_WIDTH)),
            _const_spec((D_MODEL, D_MODEL)), vec(), vec(),
            _const_spec((D_MODEL, D_FF)), _const_spec((D_FF, D_MODEL)), vec(), vec()],
        out_specs=pl.BlockSpec((DENSE_TILE, D_MODEL), lambda i: (i, 0)),
        out_shape=jax.ShapeDtypeStruct((n_tiles * DENSE_TILE, D_MODEL), F32),
        compiler_params=pltpu.CompilerParams(
            dimension_semantics=("parallel",), vmem_limit_bytes=VMEM_LIMIT),
        name="dense",
    )(*x_parts, hm, pm, pm, pm, pm, conv_w, wo, g1, b1, w1, w2, g2, b2)


def kernel(x_prompt, x_sample, w_in, b_gate, mh_norm_w, conv_w, w_out, ln1_g, ln1_b, w_ff1, w_ff2, ln2_g, ln2_b):
    seq_lens = [x_prompt.shape[1]] * x_prompt.shape[0] + [x_sample.shape[1]] * x_sample.shape[0]
    assert all(s % ROW_TILE == 0 and s % DENSE_SUB == 0 for s in seq_lens)
    assert sum(seq_lens) % MIX_TILE == 0 and sum(seq_lens) % DENSE_TILE == 0
    starts = [sum(seq_lens[:i]) for i in range(len(seq_lens))]
    first_chunks = tuple(s // CHUNK for s in starts)
    last_chunks = tuple((s + n) // CHUNK - 1 for s, n in zip(starts, seq_lens))
    first_subs = tuple(s // DENSE_SUB for s in starts)
    last_subs = tuple((s + n) // DENSE_SUB - 1 for s, n in zip(starts, seq_lens))

    x_parts = (x_prompt.reshape(-1, D_MODEL), x_sample.reshape(-1, D_MODEL))
    n_prompt = x_parts[0].shape[0]

    k_lo, v_lo = MLSTM_WIDTH, 2 * MLSTM_WIDTH
    gate_lo = 4 * MLSTM_WIDTH
    gate_hi = gate_lo + 2 * N_HD
    for l in range(DEPTH):
        w_gate = (w_in[l, :, gate_lo:gate_hi].reshape(D_MODEL, N_DIR, 2, N_HEADS)
                  .transpose(0, 2, 1, 3).reshape(D_MODEL, 2 * N_HD))
        w_gate = jnp.pad(w_gate, ((0, 0), (0, GATE_PAD - 2 * N_HD))).astype(BF16)
        w_main = jnp.concatenate(
            [w_in[l, :, :k_lo], w_in[l, :, v_lo:gate_lo], w_in[l, :, gate_hi:]], axis=1).astype(BF16)
        w_kt = w_in[l, :, k_lo:v_lo].T.astype(BF16)
        bias = b_gate[l].astype(F32).transpose(1, 0, 2).reshape(2 * N_HD, 1)
        pm, kt, grow, gcol = _proj_call(x_parts, w_gate, w_main, w_kt, bias)
        stf, mf, stb, mb = _state_call(pm, kt, grow, first_chunks, last_chunks)
        hm = _mixer_call(pm, kt, grow, gcol, stf, mf, stb, mb, mh_norm_w[l].astype(F32).reshape(1, -1))
        dense = functools.partial(
            _dense_call, x_parts, hm, pm, conv_w[l].astype(F32), w_out[l].astype(BF16),
            ln1_g[l].reshape(1, -1), ln1_b[l].reshape(1, -1), w_ff1[l].astype(BF16), w_ff2[l].astype(BF16),
            ln2_g[l].reshape(1, -1), ln2_b[l].reshape(1, -1), first_subs, last_subs)
        if l + 1 < DEPTH or len(x_parts) != 1 or n_prompt % DENSE_TILE:
            x_parts = (dense(),)
        else:
            n_tiles = hm.shape[0] // DENSE_TILE
            split = n_prompt // DENSE_TILE
            return (dense(tile_range=(0, split)).reshape(x_prompt.shape),
                    dense(tile_range=(split, n_tiles - split)).reshape(x_sample.shape))

    x = x_parts[0]
    return (x[:n_prompt].reshape(x_prompt.shape), x[n_prompt:].reshape(x_sample.shape))
```

```python
import functools

import jax
import jax.numpy as jnp
from jax import lax
from jax.experimental import pallas as pl
from jax.experimental.pallas import tpu as pltpu

D_MODEL = 1024
DEPTH = 4
N_HEADS = 4
HEAD_DIM = 128
MLSTM_WIDTH = N_HEADS * HEAD_DIM
CONV_WIDTH = D_MODEL - MLSTM_WIDTH
N_DIR = 2
N_HD = N_DIR * N_HEADS
D_FF = 4 * D_MODEL
ALPHA = (2.0 * DEPTH) ** 0.25
LN_EPS = 1e-5
Q_SCALE = HEAD_DIM ** -0.5
LOG2E = 1.4426950408889634

LANES = 128
COL_BLOCK = 512
Q_BLK, V_BLK, O_BLK, CB_BLK, U_BLK = range(5)
PM_WIDTH = 5 * COL_BLOCK
GATE_PAD = LANES

CHUNK = 128
ROW_TILE = 512
N_SUB = ROW_TILE // CHUNK
MIX_TILE = 1024
FF_SPLIT = 4
DENSE_TILE = 1024
DENSE_SUB = 512
BF16_ROWS = 16
HALO_ROWS = BF16_ROWS
VMEM_LIMIT = 62 * 1024 * 1024

R_ROW, A_ROW, G_ROW, AMAX_ROW = 0, N_HD, 2 * N_HD, 3 * N_HD
GROW_ROWS = 4 * N_HD
C_TERMS = 2
B_TERMS = 3
GCOL_COLS = (C_TERMS + B_TERMS) * N_HD

F32 = jnp.float32
BF16 = jnp.bfloat16


def _const_spec(shape):
    nd = len(shape)
    return pl.BlockSpec(shape, lambda *_: (0,) * nd, pipeline_mode=pl.Buffered(1))


def _layer_norm(x, g, b):
    mu = jnp.mean(x, axis=-1, keepdims=True)
    xc = x - mu
    var = jnp.mean(xc * xc, axis=-1, keepdims=True)
    return xc * lax.rsqrt(var + LN_EPS) * g + b


def _log_sigmoid(x):
    return jnp.minimum(x, 0.0) - jnp.log1p(jnp.exp(-jnp.abs(x)))


def _is_one_of(idx, values):
    hit = idx == values[0]
    for v in values[1:]:
        hit = jnp.logical_or(hit, idx == v)
    return hit


def _chunk_scan(x, combine, neutral, fwd_row, pos):
    width = x.shape[1]
    s = 1
    while s < CHUNK:
        from_left = pltpu.roll(x, s, axis=1)
        from_right = pltpu.roll(x, width - s, axis=1)
        take_left = jnp.logical_and(fwd_row, pos >= s)
        take_right = jnp.logical_and(jnp.logical_not(fwd_row), pos < CHUNK - s)
        x = combine(x, jnp.where(take_left, from_left, jnp.where(take_right, from_right, neutral)))
        s *= 2
    return x


def _proj_kernel(x_ref, wg_ref, wq_ref, wk_ref, wvo_ref, wc_ref, bias_ref, wo_ref, w1_ref, w2_ref,
                 pm_ref, kt_ref, grow_ref, gcol_ref, wo_out_ref, w1_out_ref, w2_out_ref):
    wo_out_ref[...] = wo_ref[...].astype(BF16)
    w1_out_ref[...] = w1_ref[...].astype(BF16)
    w2_out_ref[...] = w2_ref[...].astype(BF16)
    xb = x_ref[...].astype(BF16)
    g_tok = jnp.dot(xb, wg_ref[...], preferred_element_type=F32)
    k = jnp.dot(xb, wk_ref[...].astype(BF16), preferred_element_type=F32)
    kt_ref[...] = k.T.astype(BF16)
    q = jnp.dot(xb, wq_ref[...].astype(BF16), preferred_element_type=F32)
    pm_ref[:, :COL_BLOCK] = (q * Q_SCALE).astype(BF16)
    vo = jnp.dot(xb, wvo_ref[...].astype(BF16), preferred_element_type=F32)
    pm_ref[:, V_BLK * COL_BLOCK:CB_BLK * COL_BLOCK] = vo.astype(BF16)
    c = jnp.dot(xb, wc_ref[...], preferred_element_type=F32)
    pm_ref[:, CB_BLK * COL_BLOCK:U_BLK * COL_BLOCK] = c[:, :COL_BLOCK].astype(BF16)
    pm_ref[:, U_BLK * COL_BLOCK:] = (c[:, COL_BLOCK:2 * COL_BLOCK] * c[:, 2 * COL_BLOCK:]).astype(BF16)

    g_row = g_tok.T[:2 * N_HD, :] + bias_ref[...]
    ig = g_row[:N_HD, :]
    lf = _log_sigmoid(g_row[N_HD:, :])
    fwd_row = lax.broadcasted_iota(jnp.int32, (N_HD, ROW_TILE), 0) < N_HEADS
    pos = jnp.bitwise_and(lax.broadcasted_iota(jnp.int32, (N_HD, ROW_TILE), 1), CHUNK - 1)
    b = _chunk_scan(lf, jnp.add, 0.0, fwd_row, pos)
    r = ig - b
    cm = _chunk_scan(r, jnp.maximum, -jnp.inf, fwd_row, pos)
    at_end = pos == jnp.where(fwd_row, CHUNK - 1, 0)
    b_end = jnp.where(at_end, b, 0.0)
    g_parts, amax_parts = [], []
    for j in range(N_SUB):
        sl = slice(j * CHUNK, (j + 1) * CHUNK)
        g_j = jnp.sum(b_end[:, sl], axis=1, keepdims=True)
        amax_j = g_j + jnp.max(r[:, sl], axis=1, keepdims=True)
        g_parts.append(jnp.broadcast_to(g_j, (N_HD, CHUNK)))
        amax_parts.append(jnp.broadcast_to(amax_j, (N_HD, CHUNK)))
    g_rep = jnp.concatenate(g_parts, axis=1)
    grow_ref[R_ROW:R_ROW + N_HD, :] = r * LOG2E
    grow_ref[A_ROW:A_ROW + N_HD, :] = g_rep + r
    grow_ref[G_ROW:G_ROW + N_HD, :] = g_rep
    grow_ref[AMAX_ROW:AMAX_ROW + N_HD, :] = jnp.concatenate(amax_parts, axis=1)

    terms, rest = [], b * LOG2E
    for _ in range(B_TERMS):
        term = rest.astype(BF16).astype(F32)
        terms.append(term)
        rest = rest - term
    col_src = jnp.concatenate(
        [cm * LOG2E] * C_TERMS + terms + [jnp.zeros((LANES - GCOL_COLS, ROW_TILE), F32)], axis=0)
    gcol_ref[...] = col_src.T[:, :GCOL_COLS]


def _proj_call(x_parts, layer, w_in, wg, wc, bias, w_out, w_ff1, w_ff2):
    t = sum(xp.shape[0] for xp in x_parts)
    n_first = x_parts[0].shape[0] // ROW_TILE
    if len(x_parts) == 1:
        x_specs = [pl.BlockSpec((ROW_TILE, D_MODEL), lambda i: (i, 0))]
        body = _proj_kernel
    else:
        x_specs = [pl.BlockSpec((ROW_TILE, D_MODEL), lambda i: (jnp.minimum(i, n_first - 1), 0)),
                   pl.BlockSpec((ROW_TILE, D_MODEL), lambda i: (jnp.maximum(i - n_first, 0), 0))]
        body = functools.partial(_two_source, _proj_kernel, n_first)
    w_block = lambda width, col: pl.BlockSpec(
        (None, D_MODEL, width), lambda i: (layer, 0, col), pipeline_mode=pl.Buffered(1))
    n_steps = t // ROW_TILE
    cast_in = lambda w: pl.BlockSpec((None, w.shape[1] // n_steps, w.shape[2]), lambda i: (layer, i, 0))
    cast_out = lambda w: pl.BlockSpec((w.shape[1] // n_steps, w.shape[2]), lambda i: (i, 0))
    cast_ws = (w_out, w_ff1, w_ff2)
    assert all(w.shape[1] % (n_steps * BF16_ROWS) == 0 for w in cast_ws)
    return pl.pallas_call(
        body,
        grid=(t // ROW_TILE,),
        in_specs=x_specs + [
            _const_spec((D_MODEL, GATE_PAD)),
            w_block(COL_BLOCK, 0),
            w_block(COL_BLOCK, 1),
            w_block(2 * COL_BLOCK, 1),
            _const_spec((D_MODEL, 3 * COL_BLOCK)),
            _const_spec((2 * N_HD, 1)),
        ] + [cast_in(w) for w in cast_ws],
        out_specs=[
            pl.BlockSpec((ROW_TILE, PM_WIDTH), lambda i: (i, 0)),
            pl.BlockSpec((COL_BLOCK, ROW_TILE), lambda i: (0, i)),
            pl.BlockSpec((GROW_ROWS, ROW_TILE), lambda i: (0, i)),
            pl.BlockSpec((ROW_TILE, GCOL_COLS), lambda i: (i, 0)),
        ] + [cast_out(w) for w in cast_ws],
        out_shape=[
            jax.ShapeDtypeStruct((t, PM_WIDTH), BF16),
            jax.ShapeDtypeStruct((COL_BLOCK, t), BF16),
            jax.ShapeDtypeStruct((GROW_ROWS, t), F32),
            jax.ShapeDtypeStruct((t, GCOL_COLS), F32),
        ] + [jax.ShapeDtypeStruct(w.shape[1:], BF16) for w in cast_ws],
        compiler_params=pltpu.CompilerParams(
            dimension_semantics=("parallel",), vmem_limit_bytes=VMEM_LIMIT),
        name="proj",
    )(*x_parts, wg, w_in, w_in, w_in, wc, bias, *cast_ws)


class _PickedRef:
    def __init__(self, first, second, use_first):
        self._first, self._second, self._use_first = first, second, use_first

    def __getitem__(self, idx):
        return jnp.where(self._use_first, self._first[idx], self._second[idx])


def _two_source(body, n_first, xa_ref, xb_ref, *rest):
    body(_PickedRef(xa_ref, xb_ref, pl.program_id(0) < n_first), *rest)


def _state_kernel(first_chunks, last_chunks,
                  ktf_ref, vf_ref, gf_ref, ktb_ref, vb_ref, gb_ref,
                  stf_ref, mf_ref, stb_ref, mb_ref,
                  s_scr, m_scr):
    i = pl.program_id(0)
    n_tiles = pl.num_programs(0)

    @pl.when(i == 0)
    def _():
        s_scr[...] = jnp.zeros_like(s_scr)
        m_scr[...] = jnp.zeros_like(m_scr)

    pair_row = lax.broadcasted_iota(jnp.int32, (N_HD, CHUNK), 0)
    fwd_row = pair_row < N_HEADS
    diag = pair_row == jnp.bitwise_and(lax.broadcasted_iota(jnp.int32, (N_HD, CHUNK), 1), N_HD - 1)
    ones = jnp.ones((CHUNK, HEAD_DIM), BF16)
    kt_refs, v_refs, st_refs = (ktf_ref, ktb_ref), (vf_ref, vb_ref), (stf_ref, stb_ref)

    for step in range(N_SUB):
        sub = (step, N_SUB - 1 - step)
        lanes = tuple(slice(s * CHUNK, (s + 1) * CHUNK) for s in sub)
        chunk_f = i * N_SUB + sub[0]
        chunk_b = (n_tiles - 1 - i) * N_SUB + sub[1]
        reset = (_is_one_of(chunk_f, first_chunks), _is_one_of(chunk_b, last_chunks))

        def both(row0):
            return jnp.where(fwd_row, gf_ref[row0:row0 + N_HD, lanes[0]],
                             gb_ref[row0:row0 + N_HD, lanes[1]])

        a, g, amax = both(A_ROW), both(G_ROW), both(AMAX_ROW)
        m_old = m_scr[...]
        m_prev = jnp.where(fwd_row, jnp.where(reset[0], 0.0, m_old), jnp.where(reset[1], 0.0, m_old))
        m_new = jnp.maximum(g + m_prev, amax)
        wa = jnp.exp(a - m_new)
        wc = jnp.exp(g + m_prev - m_new)
        m_scr[...] = m_new
        m_lane = jnp.sum(jnp.where(diag, m_prev, 0.0), axis=0, keepdims=True)
        m_both = jnp.concatenate([m_prev, jnp.broadcast_to(m_lane, (N_HD, CHUNK))], axis=0)
        mf_ref[sub[0]] = m_both
        mb_ref[sub[1]] = m_both

        for d in range(N_DIR):
            for h in range(N_HEADS):
                j = d * N_HEADS + h
                hs = slice(h * HEAD_DIM, (h + 1) * HEAD_DIM)
                rows = slice(sub[d] * CHUNK, (sub[d] + 1) * CHUNK)
                s_prev = jnp.where(reset[d], 0.0, s_scr[j])
                st_refs[d][sub[d], h] = s_prev.astype(BF16)
                kt_w = (kt_refs[d][hs, lanes[d]].astype(F32) * wa[j:j + 1, :]).astype(BF16)
                v_aug = jnp.concatenate([v_refs[d][rows, hs], ones], axis=1)
                upd = jnp.dot(kt_w, v_aug, preferred_element_type=F32)
                wc_j = wc[j:j + 1, :]
                s_scr[j] = jnp.concatenate([wc_j, wc_j], axis=1) * s_prev + upd


def _state_call(pm, kt, grow, first_chunks, last_chunks):
    t = pm.shape[0]
    n = t // ROW_TILE
    n_chunks = t // CHUNK
    rev = lambda i: n - 1 - i
    st_block = (N_SUB, N_HEADS, HEAD_DIM, 2 * HEAD_DIM)
    m_block = (N_SUB, 2 * N_HD, CHUNK)
    st_shape = jax.ShapeDtypeStruct((n_chunks, N_HEADS, HEAD_DIM, 2 * HEAD_DIM), BF16)
    m_shape = jax.ShapeDtypeStruct((n_chunks, 2 * N_HD, CHUNK), F32)
    return pl.pallas_call(
        functools.partial(_state_kernel, first_chunks, last_chunks),
        grid=(n,),
        in_specs=[
            pl.BlockSpec((COL_BLOCK, ROW_TILE), lambda i: (0, i)),
            pl.BlockSpec((ROW_TILE, COL_BLOCK), lambda i: (i, V_BLK)),
            pl.BlockSpec((GROW_ROWS, ROW_TILE), lambda i: (0, i)),
            pl.BlockSpec((COL_BLOCK, ROW_TILE), lambda i: (0, rev(i))),
            pl.BlockSpec((ROW_TILE, COL_BLOCK), lambda i: (rev(i), V_BLK)),
            pl.BlockSpec((GROW_ROWS, ROW_TILE), lambda i: (0, rev(i))),
        ],
        out_specs=[
            pl.BlockSpec(st_block, lambda i: (i, 0, 0, 0)),
            pl.BlockSpec(m_block, lambda i: (i, 0, 0)),
            pl.BlockSpec(st_block, lambda i: (rev(i), 0, 0, 0)),
            pl.BlockSpec(m_block, lambda i: (rev(i), 0, 0)),
        ],
        out_shape=[st_shape, m_shape, st_shape, m_shape],
        scratch_shapes=[
            pltpu.VMEM((N_HD, HEAD_DIM, 2 * HEAD_DIM), F32),
            pltpu.VMEM((N_HD, CHUNK), F32),
        ],
        compiler_params=pltpu.CompilerParams(
            dimension_semantics=("arbitrary",), vmem_limit_bytes=VMEM_LIMIT),
        name="state",
    )(kt, pm, grow, kt, pm, grow)


def _mixer_kernel(q_ref, v_ref, o_ref, kt_ref,
                  r_ref, gcol_ref, stf_ref, mf_ref, stb_ref, mb_ref, spread_ref, nw_ref, out_ref):
    row = lax.broadcasted_iota(jnp.int32, (CHUNK, CHUNK), 0)
    col = lax.broadcasted_iota(jnp.int32, (CHUNK, CHUNK), 1)
    masks = (col <= row, col >= row)
    st_refs = (stf_ref, stb_ref)
    m_refs = (mf_ref, mb_ref)
    ones = jnp.ones((CHUNK, HEAD_DIM), BF16)
    gate_lane = lax.broadcasted_iota(jnp.int32, (CHUNK, GCOL_COLS), 1)
    pair_lane = lax.broadcasted_iota(jnp.int32, (1, GCOL_COLS), 1)

    for sub in range(MIX_TILE // CHUNK):
        rows = slice(sub * CHUNK, (sub + 1) * CHUNK)
        m_lane = LOG2E * jnp.where(jnp.bitwise_and(pair_lane, N_HD - 1) < N_HEADS,
                                   mf_ref[sub, N_HD:N_HD + 1, :GCOL_COLS],
                                   mb_ref[sub, N_HD:N_HD + 1, :GCOL_COLS])
        gates = gcol_ref[rows, :]
        c_t = jnp.maximum(gates, m_lane)
        c_hi = c_t.astype(BF16).astype(F32)
        gates = jnp.where(gate_lane < N_HD, c_hi,
                          jnp.where(gate_lane < C_TERMS * N_HD, c_t - c_hi, gates)).astype(BF16)
        for h in range(N_HEADS):
            hs = slice(h * HEAD_DIM, (h + 1) * HEAD_DIM)
            qh = q_ref[rows, hs]
            qh32 = qh.astype(F32)
            s_qk = jnp.dot(qh, kt_ref[hs, rows], preferred_element_type=F32)
            v_aug = jnp.concatenate([v_ref[rows, hs], ones], axis=1)
            h_sum = None
            for d in range(N_DIR):
                j = d * N_HEADS + h
                spread = jnp.dot(gates, spread_ref[j], preferred_element_type=F32)
                c_b = spread[:, :CHUNK]
                m_prev = LOG2E * m_refs[d][sub, j:j + 1, :]
                w_d = jnp.exp2(jnp.where(masks[d], r_ref[j:j + 1, rows] - c_b, -jnp.inf))
                s_w = (s_qk * w_d).astype(BF16)
                q_w = (qh32 * jnp.exp2(m_prev - c_b)).astype(BF16)
                tot = jnp.dot(jnp.concatenate([s_w, q_w], axis=1),
                              jnp.concatenate([v_aug, st_refs[d][sub, h]], axis=0),
                              preferred_element_type=F32)
                floor = jnp.exp2(-spread[:, CHUNK:])
                h_dir = tot[:, :HEAD_DIM] / jnp.maximum(jnp.abs(tot[:, HEAD_DIM:]), floor)
                h_sum = h_dir if h_sum is None else h_sum + h_dir
            mu = jnp.mean(h_sum, axis=-1, keepdims=True)
            hc = h_sum - mu
            hn = hc * lax.rsqrt(jnp.mean(hc * hc, axis=-1, keepdims=True) + LN_EPS)
            hn = hn * nw_ref[:, hs]
            out_ref[rows, hs] = (jax.nn.sigmoid(o_ref[rows, hs].astype(F32)) * hn).astype(BF16)


def _mixer_call(pm, kt, grow, gcol, stf, mf, stb, mb, norm_w):
    t = pm.shape[0]
    n = t // MIX_TILE
    blk = lambda col: pl.BlockSpec((MIX_TILE, COL_BLOCK), lambda i: (i, col))
    st_spec = pl.BlockSpec((MIX_TILE // CHUNK, N_HEADS, HEAD_DIM, 2 * HEAD_DIM), lambda i: (i, 0, 0, 0))
    m_spec = pl.BlockSpec((MIX_TILE // CHUNK, 2 * N_HD, CHUNK), lambda i: (i, 0, 0))
    return pl.pallas_call(
        _mixer_kernel,
        grid=(n,),
        in_specs=[blk(Q_BLK), blk(V_BLK), blk(O_BLK),
                  pl.BlockSpec((COL_BLOCK, MIX_TILE), lambda i: (0, i)),
                  pl.BlockSpec((N_HD, MIX_TILE), lambda i: (0, i)),
                  pl.BlockSpec((MIX_TILE, GCOL_COLS), lambda i: (i, 0)),
                  st_spec, m_spec, st_spec, m_spec,
                  _const_spec((N_HD, GCOL_COLS, 2 * CHUNK)),
                  _const_spec((1, MLSTM_WIDTH))],
        out_specs=pl.BlockSpec((MIX_TILE, MLSTM_WIDTH), lambda i: (i, 0)),
        out_shape=jax.ShapeDtypeStruct((t, MLSTM_WIDTH), BF16),
        compiler_params=pltpu.CompilerParams(
            dimension_semantics=("parallel",), vmem_limit_bytes=VMEM_LIMIT),
        name="mixer",
    )(pm, pm, pm, kt, grow, gcol, stf, mf, stb, mb, _spread_matrix(), norm_w)


def _spread_matrix():
    term = jnp.arange(GCOL_COLS)[None, :, None] // N_HD
    pair = jnp.arange(GCOL_COLS)[None, :, None] % N_HD
    want = jnp.arange(N_HD)[:, None, None]
    lane = jnp.arange(2 * CHUNK)[None, None, :]
    hit = (pair == want) & ((term < C_TERMS) | (lane >= CHUNK))
    return hit.astype(BF16)


def _dense_kernel(tile0, first_subs, last_subs,
                  x_ref, hm_ref, cb_ref, u_ref, up_ref, un_ref, cw_ref, wo_ref,
                  g1_ref, b1_ref, w1_ref, w2_ref, g2_ref, b2_ref, out_ref):
    slab = D_FF // FF_SPLIT
    n_sub = DENSE_TILE // DENSE_SUB
    rows = [slice(r * DENSE_SUB, (r + 1) * DENSE_SUB) for r in range(n_sub)]
    tok = lax.broadcasted_iota(jnp.int32, (DENSE_SUB, CONV_WIDTH), 0)

    def conv_gate(r):
        group = (pl.program_id(0) + tile0) * n_sub + r
        u = u_ref[rows[r], :].astype(F32)
        before = (up_ref[HALO_ROWS - 1:HALO_ROWS, :] if r == 0
                  else u_ref[r * DENSE_SUB - 1:r * DENSE_SUB, :]).astype(F32)
        after = (un_ref[0:1, :] if r == n_sub - 1
                 else u_ref[(r + 1) * DENSE_SUB:(r + 1) * DENSE_SUB + 1, :]).astype(F32)
        before = jnp.where(_is_one_of(group, first_subs), 0.0, before)
        after = jnp.where(_is_one_of(group, last_subs), 0.0, after)
        u_prev = jnp.where(tok == 0, before, pltpu.roll(u, 1, axis=0))
        u_next = jnp.where(tok == DENSE_SUB - 1, after, pltpu.roll(u, DENSE_SUB - 1, axis=0))
        y = cw_ref[0:1, :] * u_prev + cw_ref[1:2, :] * u + cw_ref[2:3, :] * u_next
        return (cb_ref[rows[r], :].astype(F32) * y).astype(BF16)

    def mix_out(r):
        y = jnp.dot(hm_ref[rows[r], :], wo_ref[:MLSTM_WIDTH, :], preferred_element_type=F32)
        return y + jnp.dot(conv_gate(r), wo_ref[MLSTM_WIDTH:, :], preferred_element_type=F32)

    def norm1(r, y):
        x1 = _layer_norm(ALPHA * x_ref[rows[r], :] + y, g1_ref[...], b1_ref[...])
        return x1.astype(BF16), ALPHA * x1

    def ffn_slab(j, x1b, acc):
        hid = jnp.dot(x1b, w1_ref[:, j * slab:(j + 1) * slab], preferred_element_type=F32)
        hid = jnp.square(jnp.maximum(hid, 0.0)).astype(BF16)
        return acc + jnp.dot(hid, w2_ref[j * slab:(j + 1) * slab, :], preferred_element_type=F32)

    state = [None] * n_sub
    state[0] = norm1(0, mix_out(0))
    for r in range(n_sub):
        x1b, acc = state[r]
        acc = ffn_slab(0, x1b, acc)
        if r + 1 < n_sub:
            state[r + 1] = norm1(r + 1, mix_out(r + 1))
        if r > 0:
            out_ref[rows[r - 1], :] = _layer_norm(state[r - 1], g2_ref[...], b2_ref[...])
        for j in range(1, FF_SPLIT):
            acc = ffn_slab(j, x1b, acc)
        state[r] = acc
    out_ref[rows[n_sub - 1], :] = _layer_norm(state[n_sub - 1], g2_ref[...], b2_ref[...])


def _dense_call(x_parts, hm, pm, conv_w, wo, g1, b1, w1, w2, g2, b2, first_subs, last_subs,
                tile_range=None):
    t = hm.shape[0]
    tile0, n_tiles = tile_range if tile_range is not None else (0, t // DENSE_TILE)
    n_first = x_parts[0].shape[0] // DENSE_TILE
    per_halo = DENSE_TILE // HALO_ROWS
    n_halo = t // HALO_ROWS
    row = lambda: pl.BlockSpec((DENSE_TILE, D_MODEL), lambda i: (i + tile0, 0))
    blk = lambda col: pl.BlockSpec((DENSE_TILE, COL_BLOCK), lambda i: (i + tile0, col))
    vec = lambda: _const_spec((1, D_MODEL))
    body = functools.partial(_dense_kernel, tile0, first_subs, last_subs)
    if len(x_parts) == 1:
        x_specs = [row()]
    else:
        assert tile0 == 0
        x_specs = [pl.BlockSpec((DENSE_TILE, D_MODEL), lambda i: (jnp.minimum(i, n_first - 1), 0)),
                   pl.BlockSpec((DENSE_TILE, D_MODEL), lambda i: (jnp.maximum(i - n_first, 0), 0))]
        body = functools.partial(_two_source, body, n_first)
    return pl.pallas_call(
        body,
        grid=(n_tiles,),
        in_specs=x_specs + [
            pl.BlockSpec((DENSE_TILE, MLSTM_WIDTH), lambda i: (i + tile0, 0)),
            blk(CB_BLK), blk(U_BLK),
            pl.BlockSpec((HALO_ROWS, COL_BLOCK),
                         lambda i: (jnp.maximum((i + tile0) * per_halo - 1, 0), U_BLK)),
            pl.BlockSpec((HALO_ROWS, COL_BLOCK),
                         lambda i: (jnp.minimum((i + tile0 + 1) * per_halo, n_halo - 1), U_BLK)),
            _const_spec((3, CONV_WIDTH)),
            _const_spec((D_MODEL, D_MODEL)), vec(), vec(),
            _const_spec((D_MODEL, D_FF)), _const_spec((D_FF, D_MODEL)), vec(), vec()],
        out_specs=pl.BlockSpec((DENSE_TILE, D_MODEL), lambda i: (i, 0)),
        out_shape=jax.ShapeDtypeStruct((n_tiles * DENSE_TILE, D_MODEL), F32),
        compiler_params=pltpu.CompilerParams(
            dimension_semantics=("parallel",), vmem_limit_bytes=VMEM_LIMIT),
        name="dense",
    )(*x_parts, hm, pm, pm, pm, pm, conv_w, wo, g1, b1, w1, w2, g2, b2)


def kernel(x_prompt, x_sample, w_in, b_gate, mh_norm_w, conv_w, w_out, ln1_g, ln1_b, w_ff1, w_ff2, ln2_g, ln2_b):
    seq_lens = [x_prompt.shape[1]] * x_prompt.shape[0] + [x_sample.shape[1]] * x_sample.shape[0]
    assert all(s % ROW_TILE == 0 and s % DENSE_SUB == 0 for s in seq_lens)
    assert sum(seq_lens) % MIX_TILE == 0 and sum(seq_lens) % DENSE_TILE == 0
    starts = [sum(seq_lens[:i]) for i in range(len(seq_lens))]
    first_chunks = tuple(s // CHUNK for s in starts)
    last_chunks = tuple((s + n) // CHUNK - 1 for s, n in zip(starts, seq_lens))
    first_subs = tuple(s // DENSE_SUB for s in starts)
    last_subs = tuple((s + n) // DENSE_SUB - 1 for s, n in zip(starts, seq_lens))

    x_parts = (x_prompt.reshape(-1, D_MODEL), x_sample.reshape(-1, D_MODEL))
    n_prompt = x_parts[0].shape[0]

    gate_lo = 4 * MLSTM_WIDTH
    gate_hi = gate_lo + 2 * N_HD
    assert w_in.shape[-1] == gate_hi + 3 * CONV_WIDTH
    for l in range(DEPTH):
        w_gate = (w_in[l, :, gate_lo:gate_hi].reshape(D_MODEL, N_DIR, 2, N_HEADS)
                  .transpose(0, 2, 1, 3).reshape(D_MODEL, 2 * N_HD))
        w_gate = jnp.pad(w_gate, ((0, 0), (0, GATE_PAD - 2 * N_HD))).astype(BF16)
        w_conv = w_in[l, :, gate_hi:].astype(BF16)
        bias = b_gate[l].astype(F32).transpose(1, 0, 2).reshape(2 * N_HD, 1)
        pm, kt, grow, gcol, wo_b, w1_b, w2_b = _proj_call(
            x_parts, l, w_in, w_gate, w_conv, bias, w_out, w_ff1, w_ff2)
        stf, mf, stb, mb = _state_call(pm, kt, grow, first_chunks, last_chunks)
        hm = _mixer_call(pm, kt, grow, gcol, stf, mf, stb, mb, mh_norm_w[l].astype(F32).reshape(1, -1))
        dense = functools.partial(
            _dense_call, x_parts, hm, pm, conv_w[l].astype(F32), wo_b,
            ln1_g[l].reshape(1, -1), ln1_b[l].reshape(1, -1), w1_b, w2_b,
            ln2_g[l].reshape(1, -1), ln2_b[l].reshape(1, -1), first_subs, last_subs)
        if l + 1 < DEPTH or len(x_parts) != 1 or n_prompt % DENSE_TILE:
            x_parts = (dense(),)
        else:
            n_tiles = hm.shape[0] // DENSE_TILE
            split = n_prompt // DENSE_TILE
            return (dense(tile_range=(0, split)).reshape(x_prompt.shape),
                    dense(tile_range=(split, n_tiles - split)).reshape(x_sample.shape))

    x = x_parts[0]
    return (x[:n_prompt].reshape(x_prompt.shape), x[n_prompt:].reshape(x_sample.shape))
```

```python
import functools

import jax
import jax.numpy as jnp
from jax import lax
from jax.experimental import pallas as pl
from jax.experimental.pallas import tpu as pltpu

D_MODEL = 1024
DEPTH = 4
N_HEADS = 4
HEAD_DIM = 128
MLSTM_WIDTH = N_HEADS * HEAD_DIM
CONV_WIDTH = D_MODEL - MLSTM_WIDTH
N_DIR = 2
N_HD = N_DIR * N_HEADS
D_FF = 4 * D_MODEL
ALPHA = (2.0 * DEPTH) ** 0.25
LN_EPS = 1e-5
Q_SCALE = HEAD_DIM ** -0.5
LOG2E = 1.4426950408889634

LANES = 128
COL_BLOCK = 512
Q_BLK, V_BLK, O_BLK, CB_BLK, U_BLK = range(5)
PM_WIDTH = 5 * COL_BLOCK
GATE_PAD = LANES

CHUNK = 128
ROW_TILE = 512
N_SUB = ROW_TILE // CHUNK
MIX_TILE = 1024
FF_SPLIT = 4
DENSE_TILE = 1024
DENSE_SUB = 512
BF16_ROWS = 16
HALO_ROWS = BF16_ROWS
VMEM_LIMIT = 62 * 1024 * 1024

R_ROW, A_ROW, G_ROW, AMAX_ROW = 0, N_HD, 2 * N_HD, 3 * N_HD
GROW_ROWS = 4 * N_HD
C_TERMS = 2
B_TERMS = 3
GCOL_COLS = (C_TERMS + B_TERMS) * N_HD
STATE_SHAPE = (N_HEADS, HEAD_DIM, 2 * HEAD_DIM)
M_SHAPE = (2 * N_HD, CHUNK)

F32 = jnp.float32
BF16 = jnp.bfloat16


def _const_spec(shape):
    nd = len(shape)
    return pl.BlockSpec(shape, lambda *_: (0,) * nd, pipeline_mode=pl.Buffered(1))


def _layer_norm(x, g, b):
    mu = jnp.mean(x, axis=-1, keepdims=True)
    xc = x - mu
    var = jnp.mean(xc * xc, axis=-1, keepdims=True)
    return xc * lax.rsqrt(var + LN_EPS) * g + b


def _log_sigmoid(x):
    return jnp.minimum(x, 0.0) - jnp.log1p(jnp.exp(-jnp.abs(x)))


def _is_one_of(idx, values):
    hit = idx == values[0]
    for v in values[1:]:
        hit = jnp.logical_or(hit, idx == v)
    return hit


def _gate_recurrence(a, g, amax, reset, m_scr):
    m_prev = jnp.where(reset, 0.0, m_scr[...])
    m_new = jnp.maximum(g + m_prev, amax)
    m_scr[...] = m_new
    return m_prev, jnp.exp(a - m_new), jnp.exp(g + m_prev - m_new)


def _pair_on_lanes(m_rows):
    pair_row = lax.broadcasted_iota(jnp.int32, (N_HD, CHUNK), 0)
    diag = pair_row == jnp.bitwise_and(lax.broadcasted_iota(jnp.int32, (N_HD, CHUNK), 1), N_HD - 1)
    return jnp.sum(jnp.where(diag, m_rows, 0.0), axis=0, keepdims=True)


def _advance_state(s_prev, kt_f32, wa_row, wc_row, v_aug):
    upd = jnp.dot((kt_f32 * wa_row).astype(BF16), v_aug, preferred_element_type=F32)
    return jnp.concatenate([wc_row, wc_row], axis=1) * s_prev + upd


def _chunk_scan(x, combine, neutral, fwd_row, pos):
    width = x.shape[1]
    s = 1
    while s < CHUNK:
        from_left = pltpu.roll(x, s, axis=1)
        from_right = pltpu.roll(x, width - s, axis=1)
        take_left = jnp.logical_and(fwd_row, pos >= s)
        take_right = jnp.logical_and(jnp.logical_not(fwd_row), pos < CHUNK - s)
        x = combine(x, jnp.where(take_left, from_left, jnp.where(take_right, from_right, neutral)))
        s *= 2
    return x


def _proj_kernel(first_chunks,
                 x_ref, wg_ref, wq_ref, wk_ref, wvo_ref, wc_ref, bias_ref, wo_ref, w1_ref, w2_ref,
                 pm_ref, kt_ref, grow_ref, gcol_ref, stf_ref, mf_ref, wo_out_ref, w1_out_ref, w2_out_ref,
                 s_scr, m_scr):
    i = pl.program_id(0)

    @pl.when(i == 0)
    def _():
        s_scr[...] = jnp.zeros_like(s_scr)
        m_scr[...] = jnp.zeros_like(m_scr)

    wo_out_ref[...] = wo_ref[...].astype(BF16)
    w1_out_ref[...] = w1_ref[...].astype(BF16)
    w2_out_ref[...] = w2_ref[...].astype(BF16)
    xb = x_ref[...].astype(BF16)
    g_tok = jnp.dot(xb, wg_ref[...], preferred_element_type=F32)
    kt = jnp.dot(xb, wk_ref[...].astype(BF16), preferred_element_type=F32).T
    kt_ref[...] = kt.astype(BF16)
    vo = jnp.dot(xb, wvo_ref[...].astype(BF16), preferred_element_type=F32).astype(BF16)
    pm_ref[:, V_BLK * COL_BLOCK:CB_BLK * COL_BLOCK] = vo

    g_row = g_tok.T[:2 * N_HD, :] + bias_ref[...]
    ig = g_row[:N_HD, :]
    lf = _log_sigmoid(g_row[N_HD:, :])
    fwd_row = lax.broadcasted_iota(jnp.int32, (N_HD, ROW_TILE), 0) < N_HEADS
    pos = jnp.bitwise_and(lax.broadcasted_iota(jnp.int32, (N_HD, ROW_TILE), 1), CHUNK - 1)
    b = _chunk_scan(lf, jnp.add, 0.0, fwd_row, pos)
    r = ig - b
    cm = _chunk_scan(r, jnp.maximum, -jnp.inf, fwd_row, pos)
    at_end = pos == jnp.where(fwd_row, CHUNK - 1, 0)
    b_end = jnp.where(at_end, b, 0.0)
    g_parts, amax_parts = [], []
    for j in range(N_SUB):
        sl = slice(j * CHUNK, (j + 1) * CHUNK)
        g_j = jnp.sum(b_end[:, sl], axis=1, keepdims=True)
        amax_j = g_j + jnp.max(r[:, sl], axis=1, keepdims=True)
        g_parts.append(jnp.broadcast_to(g_j, (N_HD, CHUNK)))
        amax_parts.append(jnp.broadcast_to(amax_j, (N_HD, CHUNK)))
    g_rep = jnp.concatenate(g_parts, axis=1)
    a_all = g_rep + r
    grow_ref[R_ROW:R_ROW + N_HD, :] = r * LOG2E
    grow_ref[A_ROW:A_ROW + N_HD, :] = a_all
    grow_ref[G_ROW:G_ROW + N_HD, :] = g_rep
    grow_ref[AMAX_ROW:AMAX_ROW + N_HD, :] = jnp.concatenate(amax_parts, axis=1)

    terms, rest = [], b * LOG2E
    for _ in range(B_TERMS):
        term = rest.astype(BF16).astype(F32)
        terms.append(term)
        rest = rest - term
    col_src = jnp.concatenate(
        [cm * LOG2E] * C_TERMS + terms + [jnp.zeros((LANES - GCOL_COLS, ROW_TILE), F32)], axis=0)
    gcol_ref[...] = col_src.T[:, :GCOL_COLS]

    ones = jnp.ones((CHUNK, HEAD_DIM), BF16)
    for c in range(N_SUB):
        sl = slice(c * CHUNK, (c + 1) * CHUNK)
        reset = _is_one_of(i * N_SUB + c, first_chunks)
        m_prev, wa, wc = _gate_recurrence(a_all[:, sl], g_parts[c], amax_parts[c], reset, m_scr)
        mf_ref[c] = jnp.concatenate(
            [m_prev, jnp.broadcast_to(_pair_on_lanes(m_prev), (N_HD, CHUNK))], axis=0)
        for h in range(N_HEADS):
            hs = slice(h * HEAD_DIM, (h + 1) * HEAD_DIM)
            s_prev = jnp.where(reset, 0.0, s_scr[h])
            stf_ref[c, h] = s_prev.astype(BF16)
            v_aug = jnp.concatenate([vo[sl, hs], ones], axis=1)
            s_scr[h] = _advance_state(s_prev, kt[hs, sl], wa[h:h + 1, :], wc[h:h + 1, :], v_aug)

    q = jnp.dot(xb, wq_ref[...].astype(BF16), preferred_element_type=F32)
    pm_ref[:, :COL_BLOCK] = (q * Q_SCALE).astype(BF16)
    c = jnp.dot(xb, wc_ref[...], preferred_element_type=F32)
    pm_ref[:, CB_BLK * COL_BLOCK:U_BLK * COL_BLOCK] = c[:, :COL_BLOCK].astype(BF16)
    pm_ref[:, U_BLK * COL_BLOCK:] = (c[:, COL_BLOCK:2 * COL_BLOCK] * c[:, 2 * COL_BLOCK:]).astype(BF16)


def _proj_call(x_parts, layer, w_qkvo, wg, wc, bias, w_out, w_ff1, w_ff2, first_chunks):
    t = sum(xp.shape[0] for xp in x_parts)
    n_first = x_parts[0].shape[0] // ROW_TILE
    if len(x_parts) == 1:
        x_specs = [pl.BlockSpec((ROW_TILE, D_MODEL), lambda i: (i, 0))]
        body = functools.partial(_proj_kernel, first_chunks)
    else:
        x_specs = [pl.BlockSpec((ROW_TILE, D_MODEL), lambda i: (jnp.minimum(i, n_first - 1), 0)),
                   pl.BlockSpec((ROW_TILE, D_MODEL), lambda i: (jnp.maximum(i - n_first, 0), 0))]
        body = functools.partial(_two_source, functools.partial(_proj_kernel, first_chunks), n_first)
    w_block = lambda width, col: pl.BlockSpec(
        (None, D_MODEL, width), lambda i: (layer, 0, col), pipeline_mode=pl.Buffered(1))
    n_steps = t // ROW_TILE
    cast_in = lambda w: pl.BlockSpec((None, w.shape[1] // n_steps, w.shape[2]), lambda i: (layer, i, 0))
    cast_out = lambda w: pl.BlockSpec((w.shape[1] // n_steps, w.shape[2]), lambda i: (i, 0))
    cast_ws = (w_out, w_ff1, w_ff2)
    assert all(w.shape[1] % (n_steps * BF16_ROWS) == 0 for w in cast_ws)
    return pl.pallas_call(
        body,
        grid=(t // ROW_TILE,),
        in_specs=x_specs + [
            _const_spec((D_MODEL, GATE_PAD)),
            w_block(COL_BLOCK, 0),
            w_block(COL_BLOCK, 1),
            w_block(2 * COL_BLOCK, 1),
            _const_spec((D_MODEL, 3 * COL_BLOCK)),
            _const_spec((2 * N_HD, 1)),
        ] + [cast_in(w) for w in cast_ws],
        out_specs=[
            pl.BlockSpec((ROW_TILE, PM_WIDTH), lambda i: (i, 0)),
            pl.BlockSpec((COL_BLOCK, ROW_TILE), lambda i: (0, i)),
            pl.BlockSpec((GROW_ROWS, ROW_TILE), lambda i: (0, i)),
            pl.BlockSpec((ROW_TILE, GCOL_COLS), lambda i: (i, 0)),
            pl.BlockSpec((N_SUB,) + STATE_SHAPE, lambda i: (i, 0, 0, 0)),
            pl.BlockSpec((N_SUB,) + M_SHAPE, lambda i: (i, 0, 0)),
        ] + [cast_out(w) for w in cast_ws],
        out_shape=[
            jax.ShapeDtypeStruct((t, PM_WIDTH), BF16),
            jax.ShapeDtypeStruct((COL_BLOCK, t), BF16),
            jax.ShapeDtypeStruct((GROW_ROWS, t), F32),
            jax.ShapeDtypeStruct((t, GCOL_COLS), F32),
            jax.ShapeDtypeStruct((t // CHUNK,) + STATE_SHAPE, BF16),
            jax.ShapeDtypeStruct((t // CHUNK,) + M_SHAPE, F32),
        ] + [jax.ShapeDtypeStruct(w.shape[1:], BF16) for w in cast_ws],
        scratch_shapes=[pltpu.VMEM(STATE_SHAPE, F32), pltpu.VMEM((N_HD, CHUNK), F32)],
        compiler_params=pltpu.CompilerParams(
            dimension_semantics=("arbitrary",), vmem_limit_bytes=VMEM_LIMIT),
        name="proj",
    )(*x_parts, wg, w_qkvo, w_qkvo, w_qkvo, wc, bias, *cast_ws)


class _PickedRef:
    def __init__(self, first, second, use_first):
        self._first, self._second, self._use_first = first, second, use_first

    def __getitem__(self, idx):
        return jnp.where(self._use_first, self._first[idx], self._second[idx])


def _two_source(body, n_first, xa_ref, xb_ref, *rest):
    body(_PickedRef(xa_ref, xb_ref, pl.program_id(0) < n_first), *rest)


def _mixer_kernel(last_chunks,
                  q_ref, v_ref, o_ref, kt_ref, grow_ref, gcol_ref, stf_ref, mf_ref, spread_ref, nw_ref,
                  out_ref, s_scr, m_scr, sb_all, mb_all):
    i = pl.program_id(0)
    tile = pl.num_programs(0) - 1 - i

    @pl.when(i == 0)
    def _():
        s_scr[...] = jnp.zeros_like(s_scr)
        m_scr[...] = jnp.zeros_like(m_scr)

    row = lax.broadcasted_iota(jnp.int32, (CHUNK, CHUNK), 0)
    col = lax.broadcasted_iota(jnp.int32, (CHUNK, CHUNK), 1)
    masks = (col <= row, col >= row)
    ones = jnp.ones((CHUNK, HEAD_DIM), BF16)
    gate_lane = lax.broadcasted_iota(jnp.int32, (CHUNK, GCOL_COLS), 1)
    pair_lane = lax.broadcasted_iota(jnp.int32, (1, GCOL_COLS), 1)
    n_sub = MIX_TILE // CHUNK

    for sub in reversed(range(n_sub)):
        rows = slice(sub * CHUNK, (sub + 1) * CHUNK)
        reset = _is_one_of(tile * n_sub + sub, last_chunks)
        mb_prev, wa, wc = _gate_recurrence(
            grow_ref[A_ROW:A_ROW + N_HD, rows], grow_ref[G_ROW:G_ROW + N_HD, rows],
            grow_ref[AMAX_ROW:AMAX_ROW + N_HD, rows], reset, m_scr)
        mb_all[sub] = mb_prev
        for h in range(N_HEADS):
            hs = slice(h * HEAD_DIM, (h + 1) * HEAD_DIM)
            jb = N_HEADS + h
            sb_prev = jnp.where(reset, 0.0, s_scr[h])
            sb_all[sub, h] = sb_prev.astype(BF16)
            v_aug = jnp.concatenate([v_ref[rows, hs], ones], axis=1)
            s_scr[h] = _advance_state(sb_prev, kt_ref[hs, rows].astype(F32),
                                      wa[jb:jb + 1, :], wc[jb:jb + 1, :], v_aug)

    for sub in range(n_sub):
        rows = slice(sub * CHUNK, (sub + 1) * CHUNK)
        mb_prev = mb_all[sub]
        m_rows = (mf_ref[sub, :N_HD, :], mb_prev)
        m_lane = LOG2E * jnp.where(jnp.bitwise_and(pair_lane, N_HD - 1) < N_HEADS,
                                   mf_ref[sub, N_HD:N_HD + 1, :GCOL_COLS],
                                   _pair_on_lanes(mb_prev)[:, :GCOL_COLS])
        gates = gcol_ref[rows, :]
        c_t = jnp.maximum(gates, m_lane)
        c_hi = c_t.astype(BF16).astype(F32)
        gates = jnp.where(gate_lane < N_HD, c_hi,
                          jnp.where(gate_lane < C_TERMS * N_HD, c_t - c_hi, gates)).astype(BF16)
        for h in range(N_HEADS):
            hs = slice(h * HEAD_DIM, (h + 1) * HEAD_DIM)
            qh = q_ref[rows, hs]
            qh32 = qh.astype(F32)
            s_qk = jnp.dot(qh, kt_ref[hs, rows], preferred_element_type=F32)
            v_aug = jnp.concatenate([v_ref[rows, hs], ones], axis=1)
            states = (stf_ref[sub, h], sb_all[sub, h])
            h_sum = None
            for d in range(N_DIR):
                j = d * N_HEADS + h
                spread = jnp.dot(gates, spread_ref[j], preferred_element_type=F32)
                c_b = spread[:, :CHUNK]
                m_prev = LOG2E * m_rows[d][j:j + 1, :]
                w_d = jnp.exp2(jnp.where(masks[d], grow_ref[R_ROW + j:R_ROW + j + 1, rows] - c_b, -jnp.inf))
                s_w = (s_qk * w_d).astype(BF16)
                q_w = (qh32 * jnp.exp2(m_prev - c_b)).astype(BF16)
                tot = jnp.dot(jnp.concatenate([s_w, q_w], axis=1),
                              jnp.concatenate([v_aug, states[d]], axis=0),
                              preferred_element_type=F32)
                floor = jnp.exp2(-spread[:, CHUNK:])
                h_dir = tot[:, :HEAD_DIM] / jnp.maximum(jnp.abs(tot[:, HEAD_DIM:]), floor)
                h_sum = h_dir if h_sum is None else h_sum + h_dir
            mu = jnp.mean(h_sum, axis=-1, keepdims=True)
            hc = h_sum - mu
            hn = hc * lax.rsqrt(jnp.mean(hc * hc, axis=-1, keepdims=True) + LN_EPS)
            hn = hn * nw_ref[:, hs]
            out_ref[rows, hs] = (jax.nn.sigmoid(o_ref[rows, hs].astype(F32)) * hn).astype(BF16)


def _mixer_call(pm, kt, grow, gcol, stf, mf, norm_w, last_chunks):
    t = pm.shape[0]
    n = t // MIX_TILE
    n_sub = MIX_TILE // CHUNK
    rev = lambda i: n - 1 - i
    blk = lambda col: pl.BlockSpec((MIX_TILE, COL_BLOCK), lambda i: (rev(i), col))
    return pl.pallas_call(
        functools.partial(_mixer_kernel, last_chunks),
        grid=(n,),
        in_specs=[blk(Q_BLK), blk(V_BLK), blk(O_BLK),
                  pl.BlockSpec((COL_BLOCK, MIX_TILE), lambda i: (0, rev(i))),
                  pl.BlockSpec((GROW_ROWS, MIX_TILE), lambda i: (0, rev(i))),
                  pl.BlockSpec((MIX_TILE, GCOL_COLS), lambda i: (rev(i), 0)),
                  pl.BlockSpec((n_sub,) + STATE_SHAPE, lambda i: (rev(i), 0, 0, 0)),
                  pl.BlockSpec((n_sub,) + M_SHAPE, lambda i: (rev(i), 0, 0)),
                  _const_spec((N_HD, GCOL_COLS, 2 * CHUNK)),
                  _const_spec((1, MLSTM_WIDTH))],
        out_specs=pl.BlockSpec((MIX_TILE, MLSTM_WIDTH), lambda i: (rev(i), 0)),
        out_shape=jax.ShapeDtypeStruct((t, MLSTM_WIDTH), BF16),
        scratch_shapes=[pltpu.VMEM(STATE_SHAPE, F32), pltpu.VMEM((N_HD, CHUNK), F32),
                        pltpu.VMEM((n_sub,) + STATE_SHAPE, BF16), pltpu.VMEM((n_sub, N_HD, CHUNK), F32)],
        compiler_params=pltpu.CompilerParams(
            dimension_semantics=("arbitrary",), vmem_limit_bytes=VMEM_LIMIT),
        name="mixer",
    )(pm, pm, pm, kt, grow, gcol, stf, mf, _spread_matrix(), norm_w)


def _spread_matrix():
    term = jnp.arange(GCOL_COLS)[None, :, None] // N_HD
    pair = jnp.arange(GCOL_COLS)[None, :, None] % N_HD
    want = jnp.arange(N_HD)[:, None, None]
    lane = jnp.arange(2 * CHUNK)[None, None, :]
    hit = (pair == want) & ((term < C_TERMS) | (lane >= CHUNK))
    return hit.astype(BF16)


def _dense_kernel(tile0, first_subs, last_subs,
                  x_ref, hm_ref, cb_ref, u_ref, up_ref, un_ref, cw_ref, wo_ref,
                  g1_ref, b1_ref, w1_ref, w2_ref, g2_ref, b2_ref, out_ref):
    slab = D_FF // FF_SPLIT
    n_sub = DENSE_TILE // DENSE_SUB
    rows = [slice(r * DENSE_SUB, (r + 1) * DENSE_SUB) for r in range(n_sub)]
    tok = lax.broadcasted_iota(jnp.int32, (DENSE_SUB, CONV_WIDTH), 0)

    def conv_gate(r):
        group = (pl.program_id(0) + tile0) * n_sub + r
        u = u_ref[rows[r], :].astype(F32)
        before = (up_ref[HALO_ROWS - 1:HALO_ROWS, :] if r == 0
                  else u_ref[r * DENSE_SUB - 1:r * DENSE_SUB, :]).astype(F32)
        after = (un_ref[0:1, :] if r == n_sub - 1
                 else u_ref[(r + 1) * DENSE_SUB:(r + 1) * DENSE_SUB + 1, :]).astype(F32)
        before = jnp.where(_is_one_of(group, first_subs), 0.0, before)
        after = jnp.where(_is_one_of(group, last_subs), 0.0, after)
        u_prev = jnp.where(tok == 0, before, pltpu.roll(u, 1, axis=0))
        u_next = jnp.where(tok == DENSE_SUB - 1, after, pltpu.roll(u, DENSE_SUB - 1, axis=0))
        y = cw_ref[0:1, :] * u_prev + cw_ref[1:2, :] * u + cw_ref[2:3, :] * u_next
        return (cb_ref[rows[r], :].astype(F32) * y).astype(BF16)

    def mix_out(r):
        y = jnp.dot(hm_ref[rows[r], :], wo_ref[:MLSTM_WIDTH, :], preferred_element_type=F32)
        return y + jnp.dot(conv_gate(r), wo_ref[MLSTM_WIDTH:, :], preferred_element_type=F32)

    def norm1(r, y):
        x1 = _layer_norm(ALPHA * x_ref[rows[r], :] + y, g1_ref[...], b1_ref[...])
        return x1.astype(BF16), ALPHA * x1

    def ffn_slab(j, x1b, acc):
        hid = jnp.dot(x1b, w1_ref[:, j * slab:(j + 1) * slab], preferred_element_type=F32)
        hid = jnp.square(jnp.maximum(hid, 0.0)).astype(BF16)
        return acc + jnp.dot(hid, w2_ref[j * slab:(j + 1) * slab, :], preferred_element_type=F32)

    state = [None] * n_sub
    state[0] = norm1(0, mix_out(0))
    for r in range(n_sub):
        x1b, acc = state[r]
        acc = ffn_slab(0, x1b, acc)
        if r + 1 < n_sub:
            state[r + 1] = norm1(r + 1, mix_out(r + 1))
        if r > 0:
            out_ref[rows[r - 1], :] = _layer_norm(state[r - 1], g2_ref[...], b2_ref[...])
        for j in range(1, FF_SPLIT):
            acc = ffn_slab(j, x1b, acc)
        state[r] = acc
    out_ref[rows[n_sub - 1], :] = _layer_norm(state[n_sub - 1], g2_ref[...], b2_ref[...])


def _dense_call(x_parts, hm, pm, conv_w, wo, g1, b1, w1, w2, g2, b2, first_subs, last_subs,
                tile_range=None):
    t = hm.shape[0]
    tile0, n_tiles = tile_range if tile_range is not None else (0, t // DENSE_TILE)
    n_first = x_parts[0].shape[0] // DENSE_TILE
    per_halo = DENSE_TILE // HALO_ROWS
    n_halo = t // HALO_ROWS
    row = lambda: pl.BlockSpec((DENSE_TILE, D_MODEL), lambda i: (i + tile0, 0))
    blk = lambda col: pl.BlockSpec((DENSE_TILE, COL_BLOCK), lambda i: (i + tile0, col))
    vec = lambda: _const_spec((1, D_MODEL))
    body = functools.partial(_dense_kernel, tile0, first_subs, last_subs)
    if len(x_parts) == 1:
        x_specs = [row()]
    else:
        assert tile0 == 0
        x_specs = [pl.BlockSpec((DENSE_TILE, D_MODEL), lambda i: (jnp.minimum(i, n_first - 1), 0)),
                   pl.BlockSpec((DENSE_TILE, D_MODEL), lambda i: (jnp.maximum(i - n_first, 0), 0))]
        body = functools.partial(_two_source, body, n_first)
    return pl.pallas_call(
        body,
        grid=(n_tiles,),
        in_specs=x_specs + [
            pl.BlockSpec((DENSE_TILE, MLSTM_WIDTH), lambda i: (i + tile0, 0)),
            blk(CB_BLK), blk(U_BLK),
            pl.BlockSpec((HALO_ROWS, COL_BLOCK),
                         lambda i: (jnp.maximum((i + tile0) * per_halo - 1, 0), U_BLK)),
            pl.BlockSpec((HALO_ROWS, COL_BLOCK),
                         lambda i: (jnp.minimum((i + tile0 + 1) * per_halo, n_halo - 1), U_BLK)),
            _const_spec((3, CONV_WIDTH)),
            _const_spec((D_MODEL, D_MODEL)), vec(), vec(),
            _const_spec((D_MODEL, D_FF)), _const_spec((D_FF, D_MODEL)), vec(), vec()],
        out_specs=pl.BlockSpec((DENSE_TILE, D_MODEL), lambda i: (i, 0)),
        out_shape=jax.ShapeDtypeStruct((n_tiles * DENSE_TILE, D_MODEL), F32),
        compiler_params=pltpu.CompilerParams(
            dimension_semantics=("parallel",), vmem_limit_bytes=VMEM_LIMIT),
        name="dense",
    )(*x_parts, hm, pm, pm, pm, pm, conv_w, wo, g1, b1, w1, w2, g2, b2)


def kernel(x_prompt, x_sample, w_in, b_gate, mh_norm_w, conv_w, w_out, ln1_g, ln1_b, w_ff1, w_ff2, ln2_g, ln2_b):
    seq_lens = [x_prompt.shape[1]] * x_prompt.shape[0] + [x_sample.shape[1]] * x_sample.shape[0]
    assert all(s % ROW_TILE == 0 and s % DENSE_SUB == 0 for s in seq_lens)
    assert sum(seq_lens) % MIX_TILE == 0 and sum(seq_lens) % DENSE_TILE == 0
    starts = [sum(seq_lens[:i]) for i in range(len(seq_lens))]
    first_chunks = tuple(s // CHUNK for s in starts)
    last_chunks = tuple((s + n) // CHUNK - 1 for s, n in zip(starts, seq_lens))
    first_subs = tuple(s // DENSE_SUB for s in starts)
    last_subs = tuple((s + n) // DENSE_SUB - 1 for s, n in zip(starts, seq_lens))

    x_parts = (x_prompt.reshape(-1, D_MODEL), x_sample.reshape(-1, D_MODEL))
    n_prompt = x_parts[0].shape[0]

    gate_lo = 4 * MLSTM_WIDTH
    gate_hi = gate_lo + 2 * N_HD
    assert w_in.shape[-1] == gate_hi + 3 * CONV_WIDTH
    w_qkvo = w_in[:, :, :gate_lo]
    for l in range(DEPTH):
        w_gate = (w_in[l, :, gate_lo:gate_hi].reshape(D_MODEL, N_DIR, 2, N_HEADS)
                  .transpose(0, 2, 1, 3).reshape(D_MODEL, 2 * N_HD))
        w_gate = jnp.pad(w_gate, ((0, 0), (0, GATE_PAD - 2 * N_HD))).astype(BF16)
        w_conv = w_in[l, :, gate_hi:].astype(BF16)
        bias = b_gate[l].astype(F32).transpose(1, 0, 2).reshape(2 * N_HD, 1)
        pm, kt, grow, gcol, stf, mf, wo_b, w1_b, w2_b = _proj_call(
            x_parts, l, w_qkvo, w_gate, w_conv, bias, w_out, w_ff1, w_ff2, first_chunks)
        hm = _mixer_call(pm, kt, grow, gcol, stf, mf, mh_norm_w[l].astype(F32).reshape(1, -1), last_chunks)
        dense = functools.partial(
            _dense_call, x_parts, hm, pm, conv_w[l].astype(F32), wo_b,
            ln1_g[l].reshape(1, -1), ln1_b[l].reshape(1, -1), w1_b, w2_b,
            ln2_g[l].reshape(1, -1), ln2_b[l].reshape(1, -1), first_subs, last_subs)
        if l + 1 < DEPTH or len(x_parts) != 1 or n_prompt % DENSE_TILE:
            x_parts = (dense(),)
        else:
            n_tiles = hm.shape[0] // DENSE_TILE
            split = n_prompt // DENSE_TILE
            return (dense(tile_range=(0, split)).reshape(x_prompt.shape),
                    dense(tile_range=(split, n_tiles - split)).reshape(x_sample.shape))

    x = x_parts[0]
    return (x[:n_prompt].reshape(x_prompt.shape), x[n_prompt:].reshape(x_sample.shape))
```

```python
import functools

import jax
import jax.numpy as jnp
from jax import lax
from jax.experimental import pallas as pl
from jax.experimental.pallas import tpu as pltpu

D_MODEL = 1024
DEPTH = 4
N_HEADS = 4
HEAD_DIM = 128
MLSTM_WIDTH = N_HEADS * HEAD_DIM
CONV_WIDTH = D_MODEL - MLSTM_WIDTH
N_DIR = 2
N_HD = N_DIR * N_HEADS
D_FF = 4 * D_MODEL
ALPHA = (2.0 * DEPTH) ** 0.25
LN_EPS = 1e-5
Q_SCALE = HEAD_DIM ** -0.5
LOG2E = 1.4426950408889634

LANES = 128
COL_BLOCK = 512
Q_BLK, V_BLK, O_BLK, CB_BLK, U_BLK = range(5)
PM_WIDTH = 5 * COL_BLOCK
NT_DIMS = (((1,), (1,)), ((), ()))

CHUNK = 128
ROW_TILE = 512
N_SUB = ROW_TILE // CHUNK
MIX_TILE = 1024
FF_SPLIT = 4
DENSE_TILE = 1024
DENSE_SUB = 512
BF16_ROWS = 16
HALO_ROWS = BF16_ROWS
VMEM_LIMIT = 62 * 1024 * 1024

R_ROW, A_ROW, G_ROW, AMAX_ROW = 0, N_HD, 2 * N_HD, 3 * N_HD
GROW_ROWS = 4 * N_HD
C_TERMS = 2
B_TERMS = 3
GCOL_COLS = (C_TERMS + B_TERMS) * N_HD
STATE_SHAPE = (N_HEADS, HEAD_DIM, 2 * HEAD_DIM)
M_SHAPE = (2 * N_HD, CHUNK)

F32 = jnp.float32
BF16 = jnp.bfloat16


def _const_spec(shape):
    nd = len(shape)
    return pl.BlockSpec(shape, lambda *_: (0,) * nd, pipeline_mode=pl.Buffered(1))


def _layer_norm(x, g, b):
    mu = jnp.mean(x, axis=-1, keepdims=True)
    xc = x - mu
    var = jnp.mean(xc * xc, axis=-1, keepdims=True)
    return xc * lax.rsqrt(var + LN_EPS) * g + b


def _log_sigmoid(x):
    return jnp.minimum(x, 0.0) - jnp.log1p(jnp.exp(-jnp.abs(x)))


def _is_one_of(idx, values):
    hit = idx == values[0]
    for v in values[1:]:
        hit = jnp.logical_or(hit, idx == v)
    return hit


def _gate_recurrence(a, g, amax, reset, m_scr):
    m_prev = jnp.where(reset, 0.0, m_scr[...])
    m_new = jnp.maximum(g + m_prev, amax)
    m_scr[...] = m_new
    return m_prev, jnp.exp(a - m_new), jnp.exp(g + m_prev - m_new)


def _pair_on_lanes(m_rows):
    pair_row = lax.broadcasted_iota(jnp.int32, (N_HD, CHUNK), 0)
    diag = pair_row == jnp.bitwise_and(lax.broadcasted_iota(jnp.int32, (N_HD, CHUNK), 1), N_HD - 1)
    return jnp.sum(jnp.where(diag, m_rows, 0.0), axis=0, keepdims=True)


def _advance_state(s_prev, kt_f32, wa_row, wc_row, v_aug):
    upd = jnp.dot((kt_f32 * wa_row).astype(BF16), v_aug, preferred_element_type=F32)
    return jnp.concatenate([wc_row, wc_row], axis=1) * s_prev + upd


def _chunk_scan(x, combine, neutral, fwd_row, pos):
    width = x.shape[1]
    s = 1
    while s < CHUNK:
        from_left = pltpu.roll(x, s, axis=1)
        from_right = pltpu.roll(x, width - s, axis=1)
        take_left = jnp.logical_and(fwd_row, pos >= s)
        take_right = jnp.logical_and(jnp.logical_not(fwd_row), pos < CHUNK - s)
        x = combine(x, jnp.where(take_left, from_left, jnp.where(take_right, from_right, neutral)))
        s *= 2
    return x


def _proj_kernel(first_chunks,
                 x_ref, wg_ref, wq_ref, wk_ref, wvo_ref, wc_ref, bias_ref, wo_ref, w1_ref, w2_ref,
                 pm_ref, kt_ref, grow_ref, gcol_ref, stf_ref, mf_ref, wo_out_ref, w1_out_ref, w2_out_ref,
                 s_scr, m_scr):
    i = pl.program_id(0)

    @pl.when(i == 0)
    def _():
        s_scr[...] = jnp.zeros_like(s_scr)
        m_scr[...] = jnp.zeros_like(m_scr)

    wo_out_ref[...] = wo_ref[...].astype(BF16)
    w1_out_ref[...] = w1_ref[...].astype(BF16)
    w2_out_ref[...] = w2_ref[...].astype(BF16)
    xb = x_ref[...].astype(BF16)
    g_row = lax.dot_general(wg_ref[...], xb, NT_DIMS, preferred_element_type=F32) + bias_ref[...]
    kt = lax.dot_general(wk_ref[...].astype(BF16), xb, NT_DIMS, preferred_element_type=F32)
    kt_ref[...] = kt.astype(BF16)
    vo = lax.dot_general(xb, wvo_ref[...].astype(BF16), NT_DIMS, preferred_element_type=F32)
    vo = vo.astype(BF16)
    pm_ref[:, V_BLK * COL_BLOCK:CB_BLK * COL_BLOCK] = vo

    ig = g_row[:N_HD, :]
    lf = _log_sigmoid(g_row[N_HD:, :])
    fwd_row = lax.broadcasted_iota(jnp.int32, (N_HD, ROW_TILE), 0) < N_HEADS
    pos = jnp.bitwise_and(lax.broadcasted_iota(jnp.int32, (N_HD, ROW_TILE), 1), CHUNK - 1)
    b = _chunk_scan(lf, jnp.add, 0.0, fwd_row, pos)
    r = ig - b
    cm = _chunk_scan(r, jnp.maximum, -jnp.inf, fwd_row, pos)
    at_end = pos == jnp.where(fwd_row, CHUNK - 1, 0)
    b_end = jnp.where(at_end, b, 0.0)
    g_parts, amax_parts = [], []
    for j in range(N_SUB):
        sl = slice(j * CHUNK, (j + 1) * CHUNK)
        g_j = jnp.sum(b_end[:, sl], axis=1, keepdims=True)
        amax_j = g_j + jnp.max(r[:, sl], axis=1, keepdims=True)
        g_parts.append(jnp.broadcast_to(g_j, (N_HD, CHUNK)))
        amax_parts.append(jnp.broadcast_to(amax_j, (N_HD, CHUNK)))
    g_rep = jnp.concatenate(g_parts, axis=1)
    a_all = g_rep + r
    grow_ref[R_ROW:R_ROW + N_HD, :] = r * LOG2E
    grow_ref[A_ROW:A_ROW + N_HD, :] = a_all
    grow_ref[G_ROW:G_ROW + N_HD, :] = g_rep
    grow_ref[AMAX_ROW:AMAX_ROW + N_HD, :] = jnp.concatenate(amax_parts, axis=1)

    terms, rest = [], b * LOG2E
    for _ in range(B_TERMS):
        term = rest.astype(BF16).astype(F32)
        terms.append(term)
        rest = rest - term
    col_src = jnp.concatenate(
        [cm * LOG2E] * C_TERMS + terms + [jnp.zeros((LANES - GCOL_COLS, ROW_TILE), F32)], axis=0)
    gcol_ref[...] = col_src.T[:, :GCOL_COLS]

    ones = jnp.ones((CHUNK, HEAD_DIM), BF16)
    for n in range(N_SUB):
        sl = slice(n * CHUNK, (n + 1) * CHUNK)
        reset = _is_one_of(i * N_SUB + n, first_chunks)
        m_prev, wa, wc = _gate_recurrence(a_all[:, sl], g_parts[n], amax_parts[n], reset, m_scr)
        mf_ref[n] = jnp.concatenate(
            [m_prev, jnp.broadcast_to(_pair_on_lanes(m_prev), (N_HD, CHUNK))], axis=0)
        for h in range(N_HEADS):
            hs = slice(h * HEAD_DIM, (h + 1) * HEAD_DIM)
            s_prev = jnp.where(reset, 0.0, s_scr[h])
            stf_ref[n, h] = s_prev.astype(BF16)
            v_aug = jnp.concatenate([vo[sl, hs], ones], axis=1)
            s_scr[h] = _advance_state(s_prev, kt[hs, sl], wa[h:h + 1, :], wc[h:h + 1, :], v_aug)

    q = lax.dot_general(xb, wq_ref[...].astype(BF16), NT_DIMS, preferred_element_type=F32)
    pm_ref[:, :COL_BLOCK] = (q * Q_SCALE).astype(BF16)
    c = lax.dot_general(xb, wc_ref[...], NT_DIMS, preferred_element_type=F32)
    pm_ref[:, CB_BLK * COL_BLOCK:U_BLK * COL_BLOCK] = c[:, :COL_BLOCK].astype(BF16)
    pm_ref[:, U_BLK * COL_BLOCK:] = (c[:, COL_BLOCK:2 * COL_BLOCK] * c[:, 2 * COL_BLOCK:]).astype(BF16)


def _proj_call(x_parts, layer, w_in_t, wg, wc, bias, w_out, w_ff1, w_ff2, first_chunks):
    t = sum(xp.shape[0] for xp in x_parts)
    n_first = x_parts[0].shape[0] // ROW_TILE
    if len(x_parts) == 1:
        x_specs = [pl.BlockSpec((ROW_TILE, D_MODEL), lambda i: (i, 0))]
        body = functools.partial(_proj_kernel, first_chunks)
    else:
        x_specs = [pl.BlockSpec((ROW_TILE, D_MODEL), lambda i: (jnp.minimum(i, n_first - 1), 0)),
                   pl.BlockSpec((ROW_TILE, D_MODEL), lambda i: (jnp.maximum(i - n_first, 0), 0))]
        body = functools.partial(_two_source, functools.partial(_proj_kernel, first_chunks), n_first)
    w_block = lambda rows, blk: pl.BlockSpec(
        (None, rows, D_MODEL), lambda i: (layer, blk, 0), pipeline_mode=pl.Buffered(1))
    n_steps = t // ROW_TILE
    cast_in = lambda w: pl.BlockSpec((None, w.shape[1] // n_steps, w.shape[2]), lambda i: (layer, i, 0))
    cast_out = lambda w: pl.BlockSpec((w.shape[1] // n_steps, w.shape[2]), lambda i: (i, 0))
    cast_ws = (w_out, w_ff1, w_ff2)
    assert all(w.shape[1] % (n_steps * BF16_ROWS) == 0 for w in cast_ws)
    return pl.pallas_call(
        body,
        grid=(t // ROW_TILE,),
        in_specs=x_specs + [
            _const_spec((2 * N_HD, D_MODEL)),
            w_block(COL_BLOCK, 0),
            w_block(COL_BLOCK, 1),
            w_block(2 * COL_BLOCK, 1),
            _const_spec((3 * COL_BLOCK, D_MODEL)),
            _const_spec((2 * N_HD, 1)),
        ] + [cast_in(w) for w in cast_ws],
        out_specs=[
            pl.BlockSpec((ROW_TILE, PM_WIDTH), lambda i: (i, 0)),
            pl.BlockSpec((COL_BLOCK, ROW_TILE), lambda i: (0, i)),
            pl.BlockSpec((GROW_ROWS, ROW_TILE), lambda i: (0, i)),
            pl.BlockSpec((ROW_TILE, GCOL_COLS), lambda i: (i, 0)),
            pl.BlockSpec((N_SUB,) + STATE_SHAPE, lambda i: (i, 0, 0, 0)),
            pl.BlockSpec((N_SUB,) + M_SHAPE, lambda i: (i, 0, 0)),
        ] + [cast_out(w) for w in cast_ws],
        out_shape=[
            jax.ShapeDtypeStruct((t, PM_WIDTH), BF16),
            jax.ShapeDtypeStruct((COL_BLOCK, t), BF16),
            jax.ShapeDtypeStruct((GROW_ROWS, t), F32),
            jax.ShapeDtypeStruct((t, GCOL_COLS), F32),
            jax.ShapeDtypeStruct((t // CHUNK,) + STATE_SHAPE, BF16),
            jax.ShapeDtypeStruct((t // CHUNK,) + M_SHAPE, F32),
        ] + [jax.ShapeDtypeStruct(w.shape[1:], BF16) for w in cast_ws],
        scratch_shapes=[pltpu.VMEM(STATE_SHAPE, F32), pltpu.VMEM((N_HD, CHUNK), F32)],
        compiler_params=pltpu.CompilerParams(
            dimension_semantics=("arbitrary",), vmem_limit_bytes=VMEM_LIMIT),
        name="proj",
    )(*x_parts, wg, w_in_t, w_in_t, w_in_t, wc, bias, *cast_ws)


class _PickedRef:
    def __init__(self, first, second, use_first):
        self._first, self._second, self._use_first = first, second, use_first

    def __getitem__(self, idx):
        return jnp.where(self._use_first, self._first[idx], self._second[idx])


def _two_source(body, n_first, xa_ref, xb_ref, *rest):
    body(_PickedRef(xa_ref, xb_ref, pl.program_id(0) < n_first), *rest)


def _mixer_kernel(last_chunks,
                  q_ref, v_ref, o_ref, kt_ref, grow_ref, gcol_ref, stf_ref, mf_ref, spread_ref, nw_ref,
                  out_ref, s_scr, m_scr, sb_all, mb_all):
    i = pl.program_id(0)
    tile = pl.num_programs(0) - 1 - i

    @pl.when(i == 0)
    def _():
        s_scr[...] = jnp.zeros_like(s_scr)
        m_scr[...] = jnp.zeros_like(m_scr)

    row = lax.broadcasted_iota(jnp.int32, (CHUNK, CHUNK), 0)
    col = lax.broadcasted_iota(jnp.int32, (CHUNK, CHUNK), 1)
    masks = (col <= row, col >= row)
    ones = jnp.ones((CHUNK, HEAD_DIM), BF16)
    gate_lane = lax.broadcasted_iota(jnp.int32, (CHUNK, GCOL_COLS), 1)
    pair_lane = lax.broadcasted_iota(jnp.int32, (1, GCOL_COLS), 1)
    n_sub = MIX_TILE // CHUNK

    for sub in reversed(range(n_sub)):
        rows = slice(sub * CHUNK, (sub + 1) * CHUNK)
        reset = _is_one_of(tile * n_sub + sub, last_chunks)
        mb_prev, wa, wc = _gate_recurrence(
            grow_ref[A_ROW:A_ROW + N_HD, rows], grow_ref[G_ROW:G_ROW + N_HD, rows],
            grow_ref[AMAX_ROW:AMAX_ROW + N_HD, rows], reset, m_scr)
        mb_all[sub] = mb_prev
        for h in range(N_HEADS):
            hs = slice(h * HEAD_DIM, (h + 1) * HEAD_DIM)
            jb = N_HEADS + h
            sb_prev = jnp.where(reset, 0.0, s_scr[h])
            sb_all[sub, h] = sb_prev.astype(BF16)
            v_aug = jnp.concatenate([v_ref[rows, hs], ones], axis=1)
            s_scr[h] = _advance_state(sb_prev, kt_ref[hs, rows].astype(F32),
                                      wa[jb:jb + 1, :], wc[jb:jb + 1, :], v_aug)

    for sub in range(n_sub):
        rows = slice(sub * CHUNK, (sub + 1) * CHUNK)
        mb_prev = mb_all[sub]
        m_rows = (mf_ref[sub, :N_HD, :], mb_prev)
        m_lane = LOG2E * jnp.where(jnp.bitwise_and(pair_lane, N_HD - 1) < N_HEADS,
                                   mf_ref[sub, N_HD:N_HD + 1, :GCOL_COLS],
                                   _pair_on_lanes(mb_prev)[:, :GCOL_COLS])
        gates = gcol_ref[rows, :]
        c_t = jnp.maximum(gates, m_lane)
        c_hi = c_t.astype(BF16).astype(F32)
        gates = jnp.where(gate_lane < N_HD, c_hi,
                          jnp.where(gate_lane < C_TERMS * N_HD, c_t - c_hi, gates)).astype(BF16)
        for h in range(N_HEADS):
            hs = slice(h * HEAD_DIM, (h + 1) * HEAD_DIM)
            qh = q_ref[rows, hs]
            qh32 = qh.astype(F32)
            s_qk = jnp.dot(qh, kt_ref[hs, rows], preferred_element_type=F32)
            v_aug = jnp.concatenate([v_ref[rows, hs], ones], axis=1)
            states = (stf_ref[sub, h], sb_all[sub, h])
            h_sum = None
            for d in range(N_DIR):
                j = d * N_HEADS + h
                spread = jnp.dot(gates, spread_ref[j], preferred_element_type=F32)
                c_b = spread[:, :CHUNK]
                m_prev = LOG2E * m_rows[d][j:j + 1, :]
                w_d = jnp.exp2(jnp.where(masks[d], grow_ref[R_ROW + j:R_ROW + j + 1, rows] - c_b, -jnp.inf))
                s_w = (s_qk * w_d).astype(BF16)
                q_w = (qh32 * jnp.exp2(m_prev - c_b)).astype(BF16)
                tot = jnp.dot(jnp.concatenate([s_w, q_w], axis=1),
                              jnp.concatenate([v_aug, states[d]], axis=0),
                              preferred_element_type=F32)
                floor = jnp.exp2(-spread[:, CHUNK:])
                h_dir = tot[:, :HEAD_DIM] / jnp.maximum(jnp.abs(tot[:, HEAD_DIM:]), floor)
                h_sum = h_dir if h_sum is None else h_sum + h_dir
            mu = jnp.mean(h_sum, axis=-1, keepdims=True)
            hc = h_sum - mu
            hn = hc * lax.rsqrt(jnp.mean(hc * hc, axis=-1, keepdims=True) + LN_EPS)
            hn = hn * nw_ref[:, hs]
            out_ref[rows, hs] = (jax.nn.sigmoid(o_ref[rows, hs].astype(F32)) * hn).astype(BF16)


def _mixer_call(pm, kt, grow, gcol, stf, mf, norm_w, last_chunks):
    t = pm.shape[0]
    n = t // MIX_TILE
    n_sub = MIX_TILE // CHUNK
    rev = lambda i: n - 1 - i
    blk = lambda col: pl.BlockSpec((MIX_TILE, COL_BLOCK), lambda i: (rev(i), col))
    return pl.pallas_call(
        functools.partial(_mixer_kernel, last_chunks),
        grid=(n,),
        in_specs=[blk(Q_BLK), blk(V_BLK), blk(O_BLK),
                  pl.BlockSpec((COL_BLOCK, MIX_TILE), lambda i: (0, rev(i))),
                  pl.BlockSpec((GROW_ROWS, MIX_TILE), lambda i: (0, rev(i))),
                  pl.BlockSpec((MIX_TILE, GCOL_COLS), lambda i: (rev(i), 0)),
                  pl.BlockSpec((n_sub,) + STATE_SHAPE, lambda i: (rev(i), 0, 0, 0)),
                  pl.BlockSpec((n_sub,) + M_SHAPE, lambda i: (rev(i), 0, 0)),
                  _const_spec((N_HD, GCOL_COLS, 2 * CHUNK)),
                  _const_spec((1, MLSTM_WIDTH))],
        out_specs=pl.BlockSpec((MIX_TILE, MLSTM_WIDTH), lambda i: (rev(i), 0)),
        out_shape=jax.ShapeDtypeStruct((t, MLSTM_WIDTH), BF16),
        scratch_shapes=[pltpu.VMEM(STATE_SHAPE, F32), pltpu.VMEM((N_HD, CHUNK), F32),
                        pltpu.VMEM((n_sub,) + STATE_SHAPE, BF16), pltpu.VMEM((n_sub, N_HD, CHUNK), F32)],
        compiler_params=pltpu.CompilerParams(
            dimension_semantics=("arbitrary",), vmem_limit_bytes=VMEM_LIMIT),
        name="mixer",
    )(pm, pm, pm, kt, grow, gcol, stf, mf, _spread_matrix(), norm_w)


def _spread_matrix():
    term = jnp.arange(GCOL_COLS)[None, :, None] // N_HD
    pair = jnp.arange(GCOL_COLS)[None, :, None] % N_HD
    want = jnp.arange(N_HD)[:, None, None]
    lane = jnp.arange(2 * CHUNK)[None, None, :]
    hit = (pair == want) & ((term < C_TERMS) | (lane >= CHUNK))
    return hit.astype(BF16)


def _dense_kernel(tile0, first_subs, last_subs,
                  x_ref, hm_ref, cb_ref, u_ref, up_ref, un_ref, cw_ref, wo_ref,
                  g1_ref, b1_ref, w1_ref, w2_ref, g2_ref, b2_ref, out_ref):
    slab = D_FF // FF_SPLIT
    n_sub = DENSE_TILE // DENSE_SUB
    rows = [slice(r * DENSE_SUB, (r + 1) * DENSE_SUB) for r in range(n_sub)]
    tok = lax.broadcasted_iota(jnp.int32, (DENSE_SUB, CONV_WIDTH), 0)

    def conv_gate(r):
        group = (pl.program_id(0) + tile0) * n_sub + r
        u = u_ref[rows[r], :].astype(F32)
        before = (up_ref[HALO_ROWS - 1:HALO_ROWS, :] if r == 0
                  else u_ref[r * DENSE_SUB - 1:r * DENSE_SUB, :]).astype(F32)
        after = (un_ref[0:1, :] if r == n_sub - 1
                 else u_ref[(r + 1) * DENSE_SUB:(r + 1) * DENSE_SUB + 1, :]).astype(F32)
        before = jnp.where(_is_one_of(group, first_subs), 0.0, before)
        after = jnp.where(_is_one_of(group, last_subs), 0.0, after)
        u_prev = jnp.where(tok == 0, before, pltpu.roll(u, 1, axis=0))
        u_next = jnp.where(tok == DENSE_SUB - 1, after, pltpu.roll(u, DENSE_SUB - 1, axis=0))
        y = cw_ref[0:1, :] * u_prev + cw_ref[1:2, :] * u + cw_ref[2:3, :] * u_next
        return (cb_ref[rows[r], :].astype(F32) * y).astype(BF16)

    def mix_out(r):
        y = jnp.dot(hm_ref[rows[r], :], wo_ref[:MLSTM_WIDTH, :], preferred_element_type=F32)
        return y + jnp.dot(conv_gate(r), wo_ref[MLSTM_WIDTH:, :], preferred_element_type=F32)

    def norm1(r, y):
        x1 = _layer_norm(ALPHA * x_ref[rows[r], :] + y, g1_ref[...], b1_ref[...])
        return x1.astype(BF16), ALPHA * x1

    def ffn_slab(j, x1b, acc):
        hid = jnp.dot(x1b, w1_ref[:, j * slab:(j + 1) * slab], preferred_element_type=F32)
        hid = jnp.square(jnp.maximum(hid, 0.0)).astype(BF16)
        return acc + jnp.dot(hid, w2_ref[j * slab:(j + 1) * slab, :], preferred_element_type=F32)

    state = [None] * n_sub
    state[0] = norm1(0, mix_out(0))
    for r in range(n_sub):
        x1b, acc = state[r]
        acc = ffn_slab(0, x1b, acc)
        if r + 1 < n_sub:
            state[r + 1] = norm1(r + 1, mix_out(r + 1))
        if r > 0:
            out_ref[rows[r - 1], :] = _layer_norm(state[r - 1], g2_ref[...], b2_ref[...])
        for j in range(1, FF_SPLIT):
            acc = ffn_slab(j, x1b, acc)
        state[r] = acc
    out_ref[rows[n_sub - 1], :] = _layer_norm(state[n_sub - 1], g2_ref[...], b2_ref[...])


def _dense_call(x_parts, hm, pm, conv_w, wo, g1, b1, w1, w2, g2, b2, first_subs, last_subs,
                tile_range=None):
    t = hm.shape[0]
    tile0, n_tiles = tile_range if tile_range is not None else (0, t // DENSE_TILE)
    n_first = x_parts[0].shape[0] // DENSE_TILE
    per_halo = DENSE_TILE // HALO_ROWS
    n_halo = t // HALO_ROWS
    row = lambda: pl.BlockSpec((DENSE_TILE, D_MODEL), lambda i: (i + tile0, 0))
    blk = lambda col: pl.BlockSpec((DENSE_TILE, COL_BLOCK), lambda i: (i + tile0, col))
    vec = lambda: _const_spec((1, D_MODEL))
    body = functools.partial(_dense_kernel, tile0, first_subs, last_subs)
    if len(x_parts) == 1:
        x_specs = [row()]
    else:
        assert tile0 == 0
        x_specs = [pl.BlockSpec((DENSE_TILE, D_MODEL), lambda i: (jnp.minimum(i, n_first - 1), 0)),
                   pl.BlockSpec((DENSE_TILE, D_MODEL), lambda i: (jnp.maximum(i - n_first, 0), 0))]
        body = functools.partial(_two_source, body, n_first)
    return pl.pallas_call(
        body,
        grid=(n_tiles,),
        in_specs=x_specs + [
            pl.BlockSpec((DENSE_TILE, MLSTM_WIDTH), lambda i: (i + tile0, 0)),
            blk(CB_BLK), blk(U_BLK),
            pl.BlockSpec((HALO_ROWS, COL_BLOCK),
                         lambda i: (jnp.maximum((i + tile0) * per_halo - 1, 0), U_BLK)),
            pl.BlockSpec((HALO_ROWS, COL_BLOCK),
                         lambda i: (jnp.minimum((i + tile0 + 1) * per_halo, n_halo - 1), U_BLK)),
            _const_spec((3, CONV_WIDTH)),
            _const_spec((D_MODEL, D_MODEL)), vec(), vec(),
            _const_spec((D_MODEL, D_FF)), _const_spec((D_FF, D_MODEL)), vec(), vec()],
        out_specs=pl.BlockSpec((DENSE_TILE, D_MODEL), lambda i: (i, 0)),
        out_shape=jax.ShapeDtypeStruct((n_tiles * DENSE_TILE, D_MODEL), F32),
        compiler_params=pltpu.CompilerParams(
            dimension_semantics=("parallel",), vmem_limit_bytes=VMEM_LIMIT),
        name="dense",
    )(*x_parts, hm, pm, pm, pm, pm, conv_w, wo, g1, b1, w1, w2, g2, b2)


def kernel(x_prompt, x_sample, w_in, b_gate, mh_norm_w, conv_w, w_out, ln1_g, ln1_b, w_ff1, w_ff2, ln2_g, ln2_b):
    seq_lens = [x_prompt.shape[1]] * x_prompt.shape[0] + [x_sample.shape[1]] * x_sample.shape[0]
    assert all(s % ROW_TILE == 0 and s % DENSE_SUB == 0 for s in seq_lens)
    assert sum(seq_lens) % MIX_TILE == 0 and sum(seq_lens) % DENSE_TILE == 0
    starts = [sum(seq_lens[:i]) for i in range(len(seq_lens))]
    first_chunks = tuple(s // CHUNK for s in starts)
    last_chunks = tuple((s + n) // CHUNK - 1 for s, n in zip(starts, seq_lens))
    first_subs = tuple(s // DENSE_SUB for s in starts)
    last_subs = tuple((s + n) // DENSE_SUB - 1 for s, n in zip(starts, seq_lens))

    x_parts = (x_prompt.reshape(-1, D_MODEL), x_sample.reshape(-1, D_MODEL))
    n_prompt = x_parts[0].shape[0]

    gate_lo = 4 * MLSTM_WIDTH
    gate_hi = gate_lo + 2 * N_HD
    assert w_in.shape[-1] == gate_hi + 3 * CONV_WIDTH
    w_in_t = jnp.swapaxes(w_in, 1, 2)
    for l in range(DEPTH):
        w_gate = (w_in_t[l, gate_lo:gate_hi].reshape(N_DIR, 2, N_HEADS, D_MODEL)
                  .transpose(1, 0, 2, 3).reshape(2 * N_HD, D_MODEL).astype(BF16))
        w_conv = w_in_t[l, gate_hi:].astype(BF16)
        bias = b_gate[l].astype(F32).transpose(1, 0, 2).reshape(2 * N_HD, 1)
        pm, kt, grow, gcol, stf, mf, wo_b, w1_b, w2_b = _proj_call(
            x_parts, l, w_in_t, w_gate, w_conv, bias, w_out, w_ff1, w_ff2, first_chunks)
        hm = _mixer_call(pm, kt, grow, gcol, stf, mf, mh_norm_w[l].astype(F32).reshape(1, -1), last_chunks)
        dense = functools.partial(
            _dense_call, x_parts, hm, pm, conv_w[l].astype(F32), wo_b,
            ln1_g[l].reshape(1, -1), ln1_b[l].reshape(1, -1), w1_b, w2_b,
            ln2_g[l].reshape(1, -1), ln2_b[l].reshape(1, -1), first_subs, last_subs)
        if l + 1 < DEPTH or len(x_parts) != 1 or n_prompt % DENSE_TILE:
            x_parts = (dense(),)
        else:
            n_tiles = hm.shape[0] // DENSE_TILE
            split = n_prompt // DENSE_TILE
            return (dense(tile_range=(0, split)).reshape(x_prompt.shape),
                    dense(tile_range=(split, n_tiles - split)).reshape(x_sample.shape))

    x = x_parts[0]
    return (x[:n_prompt].reshape(x_prompt.shape), x[n_prompt:].reshape(x_sample.shape))
```

```python
import functools

import jax
import jax.numpy as jnp
from jax import lax
from jax.experimental import pallas as pl
from jax.experimental.pallas import tpu as pltpu

D_MODEL = 1024
DEPTH = 4
N_HEADS = 4
HEAD_DIM = 128
MLSTM_WIDTH = N_HEADS * HEAD_DIM
CONV_WIDTH = D_MODEL - MLSTM_WIDTH
N_DIR = 2
N_HD = N_DIR * N_HEADS
D_FF = 4 * D_MODEL
ALPHA = (2.0 * DEPTH) ** 0.25
LN_EPS = 1e-5
Q_SCALE = HEAD_DIM ** -0.5
LOG2E = 1.4426950408889634

LANES = 128
COL_BLOCK = 512
Q_BLK, V_BLK, O_BLK, CB_BLK, U_BLK = range(5)
PM_WIDTH = 5 * COL_BLOCK
NT_DIMS = (((1,), (1,)), ((), ()))

CHUNK = 128
ROW_TILE = 512
N_SUB = ROW_TILE // CHUNK
MIX_TILE = 2048
FF_SPLIT = 4
DENSE_TILE = 1024
DENSE_SUB = 512
BF16_ROWS = 16
HALO_ROWS = BF16_ROWS
VMEM_LIMIT = 62 * 1024 * 1024

R_ROW, A_ROW, G_ROW, AMAX_ROW = 0, N_HD, 2 * N_HD, 3 * N_HD
GROW_ROWS = 4 * N_HD
C_TERMS = 2
B_TERMS = 3
GCOL_COLS = (C_TERMS + B_TERMS) * N_HD
STATE_SHAPE = (N_HEADS, HEAD_DIM, 2 * HEAD_DIM)
M_SHAPE = (2 * N_HD, CHUNK)

F32 = jnp.float32
BF16 = jnp.bfloat16


def _const_spec(shape):
    nd = len(shape)
    return pl.BlockSpec(shape, lambda *_: (0,) * nd, pipeline_mode=pl.Buffered(1))


def _layer_norm(x, g, b):
    mu = jnp.mean(x, axis=-1, keepdims=True)
    xc = x - mu
    var = jnp.mean(xc * xc, axis=-1, keepdims=True)
    return xc * lax.rsqrt(var + LN_EPS) * g + b


def _log_sigmoid(x):
    return jnp.minimum(x, 0.0) - jnp.log1p(jnp.exp(-jnp.abs(x)))


def _is_one_of(idx, values):
    hit = idx == values[0]
    for v in values[1:]:
        hit = jnp.logical_or(hit, idx == v)
    return hit


def _gate_recurrence(a, g, amax, reset, m_scr):
    m_prev = jnp.where(reset, 0.0, m_scr[...])
    m_new = jnp.maximum(g + m_prev, amax)
    m_scr[...] = m_new
    return m_prev, jnp.exp(a - m_new), jnp.where(reset, 0.0, jnp.exp(g + m_prev - m_new))


def _pair_on_lanes(m_rows):
    pair_row = lax.broadcasted_iota(jnp.int32, (N_HD, CHUNK), 0)
    diag = pair_row == jnp.bitwise_and(lax.broadcasted_iota(jnp.int32, (N_HD, CHUNK), 1), N_HD - 1)
    return jnp.sum(jnp.where(diag, m_rows, 0.0), axis=0, keepdims=True)


def _advance_state(s_prev, kt_f32, wa_row, wc_row, v_aug):
    upd = jnp.dot((kt_f32 * wa_row).astype(BF16), v_aug, preferred_element_type=F32)
    return jnp.concatenate([wc_row, wc_row], axis=1) * s_prev + upd


def _chunk_scan(x, combine, neutral, fwd_row, pos):
    width = x.shape[1]
    s = 1
    while s < CHUNK:
        from_left = pltpu.roll(x, s, axis=1)
        from_right = pltpu.roll(x, width - s, axis=1)
        take_left = jnp.logical_and(fwd_row, pos >= s)
        take_right = jnp.logical_and(jnp.logical_not(fwd_row), pos < CHUNK - s)
        x = combine(x, jnp.where(take_left, from_left, jnp.where(take_right, from_right, neutral)))
        s *= 2
    return x


def _proj_kernel(first_chunks,
                 x_ref, wg_ref, wq_ref, wk_ref, wvo_ref, wc_ref, bias_ref, wo_ref, w1_ref, w2_ref,
                 pm_ref, kt_ref, grow_ref, gcol_ref, stf_ref, mf_ref, wo_out_ref, w1_out_ref, w2_out_ref,
                 s_scr, m_scr):
    i = pl.program_id(0)

    @pl.when(i == 0)
    def _():
        s_scr[...] = jnp.zeros_like(s_scr)
        m_scr[...] = jnp.zeros_like(m_scr)

    wo_out_ref[...] = wo_ref[...].astype(BF16)
    w1_out_ref[...] = w1_ref[...].astype(BF16)
    w2_out_ref[...] = w2_ref[...].astype(BF16)
    xb = x_ref[...].astype(BF16)
    g_row = lax.dot_general(wg_ref[...], xb, NT_DIMS, preferred_element_type=F32) + bias_ref[...]
    kt = lax.dot_general(wk_ref[...].astype(BF16), xb, NT_DIMS, preferred_element_type=F32)
    kt_ref[...] = kt.astype(BF16)
    vo = lax.dot_general(xb, wvo_ref[...].astype(BF16), NT_DIMS, preferred_element_type=F32)
    vo = vo.astype(BF16)
    pm_ref[:, V_BLK * COL_BLOCK:CB_BLK * COL_BLOCK] = vo

    ig = g_row[:N_HD, :]
    lf = _log_sigmoid(g_row[N_HD:, :])
    fwd_row = lax.broadcasted_iota(jnp.int32, (N_HD, ROW_TILE), 0) < N_HEADS
    pos = jnp.bitwise_and(lax.broadcasted_iota(jnp.int32, (N_HD, ROW_TILE), 1), CHUNK - 1)
    b = _chunk_scan(lf, jnp.add, 0.0, fwd_row, pos)
    r = ig - b
    cm = _chunk_scan(r, jnp.maximum, -jnp.inf, fwd_row, pos)
    at_end = pos == jnp.where(fwd_row, CHUNK - 1, 0)
    b_end = jnp.where(at_end, b, 0.0)
    g_parts, amax_parts = [], []
    for j in range(N_SUB):
        sl = slice(j * CHUNK, (j + 1) * CHUNK)
        g_j = jnp.sum(b_end[:, sl], axis=1, keepdims=True)
        amax_j = g_j + jnp.max(r[:, sl], axis=1, keepdims=True)
        g_parts.append(jnp.broadcast_to(g_j, (N_HD, CHUNK)))
        amax_parts.append(jnp.broadcast_to(amax_j, (N_HD, CHUNK)))
    g_rep = jnp.concatenate(g_parts, axis=1)
    a_all = g_rep + r
    grow_ref[R_ROW:R_ROW + N_HD, :] = r * LOG2E
    grow_ref[A_ROW:A_ROW + N_HD, :] = a_all
    grow_ref[G_ROW:G_ROW + N_HD, :] = g_rep
    grow_ref[AMAX_ROW:AMAX_ROW + N_HD, :] = jnp.concatenate(amax_parts, axis=1)

    terms, rest = [], b * LOG2E
    for _ in range(B_TERMS):
        term = rest.astype(BF16).astype(F32)
        terms.append(term)
        rest = rest - term
    col_src = jnp.concatenate(
        [cm * LOG2E] * C_TERMS + terms + [jnp.zeros((LANES - GCOL_COLS, ROW_TILE), F32)], axis=0)
    gcol_ref[...] = col_src.T[:, :GCOL_COLS]

    ones = jnp.ones((CHUNK, HEAD_DIM), BF16)
    for n in range(N_SUB):
        sl = slice(n * CHUNK, (n + 1) * CHUNK)
        reset = _is_one_of(i * N_SUB + n, first_chunks)
        m_prev, wa, wc = _gate_recurrence(a_all[:, sl], g_parts[n], amax_parts[n], reset, m_scr)
        mf_ref[n] = jnp.concatenate(
            [m_prev, jnp.broadcast_to(_pair_on_lanes(m_prev), (N_HD, CHUNK))], axis=0)
        for h in range(N_HEADS):
            hs = slice(h * HEAD_DIM, (h + 1) * HEAD_DIM)
            s_old = s_scr[h]
            stf_ref[n, h] = jnp.where(reset, 0.0, s_old.astype(BF16))
            v_aug = jnp.concatenate([vo[sl, hs], ones], axis=1)
            s_scr[h] = _advance_state(s_old, kt[hs, sl], wa[h:h + 1, :], wc[h:h + 1, :], v_aug)

    q = lax.dot_general(xb, wq_ref[...].astype(BF16), NT_DIMS, preferred_element_type=F32)
    pm_ref[:, :COL_BLOCK] = (q * Q_SCALE).astype(BF16)
    c = lax.dot_general(xb, wc_ref[...], NT_DIMS, preferred_element_type=F32)
    pm_ref[:, CB_BLK * COL_BLOCK:U_BLK * COL_BLOCK] = c[:, :COL_BLOCK].astype(BF16)
    pm_ref[:, U_BLK * COL_BLOCK:] = (c[:, COL_BLOCK:2 * COL_BLOCK] * c[:, 2 * COL_BLOCK:]).astype(BF16)


def _proj_call(x_parts, layer, w_in_t, wg, wc, bias, w_out, w_ff1, w_ff2, first_chunks):
    t = sum(xp.shape[0] for xp in x_parts)
    n_first = x_parts[0].shape[0] // ROW_TILE
    if len(x_parts) == 1:
        x_specs = [pl.BlockSpec((ROW_TILE, D_MODEL), lambda i: (i, 0))]
        body = functools.partial(_proj_kernel, first_chunks)
    else:
        x_specs = [pl.BlockSpec((ROW_TILE, D_MODEL), lambda i: (jnp.minimum(i, n_first - 1), 0)),
                   pl.BlockSpec((ROW_TILE, D_MODEL), lambda i: (jnp.maximum(i - n_first, 0), 0))]
        body = functools.partial(_two_source, functools.partial(_proj_kernel, first_chunks), n_first)
    w_block = lambda rows, blk: pl.BlockSpec(
        (None, rows, D_MODEL), lambda i: (layer, blk, 0), pipeline_mode=pl.Buffered(1))
    n_steps = t // ROW_TILE
    cast_in = lambda w: pl.BlockSpec((None, w.shape[1] // n_steps, w.shape[2]), lambda i: (layer, i, 0))
    cast_out = lambda w: pl.BlockSpec((w.shape[1] // n_steps, w.shape[2]), lambda i: (i, 0))
    cast_ws = (w_out, w_ff1, w_ff2)
    assert all(w.shape[1] % (n_steps * BF16_ROWS) == 0 for w in cast_ws)
    return pl.pallas_call(
        body,
        grid=(t // ROW_TILE,),
        in_specs=x_specs + [
            _const_spec((2 * N_HD, D_MODEL)),
            w_block(COL_BLOCK, 0),
            w_block(COL_BLOCK, 1),
            w_block(2 * COL_BLOCK, 1),
            _const_spec((3 * COL_BLOCK, D_MODEL)),
            _const_spec((2 * N_HD, 1)),
        ] + [cast_in(w) for w in cast_ws],
        out_specs=[
            pl.BlockSpec((ROW_TILE, PM_WIDTH), lambda i: (i, 0)),
            pl.BlockSpec((COL_BLOCK, ROW_TILE), lambda i: (0, i)),
            pl.BlockSpec((GROW_ROWS, ROW_TILE), lambda i: (0, i)),
            pl.BlockSpec((ROW_TILE, GCOL_COLS), lambda i: (i, 0)),
            pl.BlockSpec((N_SUB,) + STATE_SHAPE, lambda i: (i, 0, 0, 0)),
            pl.BlockSpec((N_SUB,) + M_SHAPE, lambda i: (i, 0, 0)),
        ] + [cast_out(w) for w in cast_ws],
        out_shape=[
            jax.ShapeDtypeStruct((t, PM_WIDTH), BF16),
            jax.ShapeDtypeStruct((COL_BLOCK, t), BF16),
            jax.ShapeDtypeStruct((GROW_ROWS, t), F32),
            jax.ShapeDtypeStruct((t, GCOL_COLS), F32),
            jax.ShapeDtypeStruct((t // CHUNK,) + STATE_SHAPE, BF16),
            jax.ShapeDtypeStruct((t // CHUNK,) + M_SHAPE, F32),
        ] + [jax.ShapeDtypeStruct(w.shape[1:], BF16) for w in cast_ws],
        scratch_shapes=[pltpu.VMEM(STATE_SHAPE, F32), pltpu.VMEM((N_HD, CHUNK), F32)],
        compiler_params=pltpu.CompilerParams(
            dimension_semantics=("arbitrary",), vmem_limit_bytes=VMEM_LIMIT),
        name="proj",
    )(*x_parts, wg, w_in_t, w_in_t, w_in_t, wc, bias, *cast_ws)


class _PickedRef:
    def __init__(self, first, second, use_first):
        self._first, self._second, self._use_first = first, second, use_first

    def __getitem__(self, idx):
        return jnp.where(self._use_first, self._first[idx], self._second[idx])


def _two_source(body, n_first, xa_ref, xb_ref, *rest):
    body(_PickedRef(xa_ref, xb_ref, pl.program_id(0) < n_first), *rest)


def _mixer_kernel(last_chunks,
                  q_ref, v_ref, o_ref, kt_ref, grow_ref, gcol_ref, stf_ref, mf_ref, spread_ref, nw_ref,
                  out_ref, s_scr, m_scr, sb_all, mb_all):
    i = pl.program_id(0)
    tile = pl.num_programs(0) - 1 - i

    @pl.when(i == 0)
    def _():
        s_scr[...] = jnp.zeros_like(s_scr)
        m_scr[...] = jnp.zeros_like(m_scr)

    row = lax.broadcasted_iota(jnp.int32, (CHUNK, CHUNK), 0)
    col = lax.broadcasted_iota(jnp.int32, (CHUNK, CHUNK), 1)
    masks = (col <= row, col >= row)
    ones = jnp.ones((CHUNK, HEAD_DIM), BF16)
    gate_lane = lax.broadcasted_iota(jnp.int32, (CHUNK, GCOL_COLS), 1)
    pair_lane = lax.broadcasted_iota(jnp.int32, (1, GCOL_COLS), 1)
    n_sub = MIX_TILE // CHUNK

    for sub in reversed(range(n_sub)):
        rows = slice(sub * CHUNK, (sub + 1) * CHUNK)
        reset = _is_one_of(tile * n_sub + sub, last_chunks)
        mb_prev, wa, wc = _gate_recurrence(
            grow_ref[A_ROW:A_ROW + N_HD, rows], grow_ref[G_ROW:G_ROW + N_HD, rows],
            grow_ref[AMAX_ROW:AMAX_ROW + N_HD, rows], reset, m_scr)
        mb_all[sub] = mb_prev
        for h in range(N_HEADS):
            hs = slice(h * HEAD_DIM, (h + 1) * HEAD_DIM)
            jb = N_HEADS + h
            s_old = s_scr[h]
            sb_all[sub, h] = jnp.where(reset, 0.0, s_old.astype(BF16))
            v_aug = jnp.concatenate([v_ref[rows, hs], ones], axis=1)
            s_scr[h] = _advance_state(s_old, kt_ref[hs, rows].astype(F32),
                                      wa[jb:jb + 1, :], wc[jb:jb + 1, :], v_aug)

    for sub in range(n_sub):
        rows = slice(sub * CHUNK, (sub + 1) * CHUNK)
        mb_prev = mb_all[sub]
        m_rows = (mf_ref[sub, :N_HD, :], mb_prev)
        m_lane = LOG2E * jnp.where(jnp.bitwise_and(pair_lane, N_HD - 1) < N_HEADS,
                                   mf_ref[sub, N_HD:N_HD + 1, :GCOL_COLS],
                                   _pair_on_lanes(mb_prev)[:, :GCOL_COLS])
        gates = gcol_ref[rows, :]
        c_t = jnp.maximum(gates, m_lane)
        c_hi = c_t.astype(BF16).astype(F32)
        gates = jnp.where(gate_lane < N_HD, c_hi,
                          jnp.where(gate_lane < C_TERMS * N_HD, c_t - c_hi, gates)).astype(BF16)
        for h in range(N_HEADS):
            hs = slice(h * HEAD_DIM, (h + 1) * HEAD_DIM)
            qh = q_ref[rows, hs]
            qh32 = qh.astype(F32)
            s_qk = jnp.dot(qh, kt_ref[hs, rows], preferred_element_type=F32)
            v_aug = jnp.concatenate([v_ref[rows, hs], ones], axis=1)
            states = (stf_ref[sub, h], sb_all[sub, h])
            h_sum = None
            for d in range(N_DIR):
                j = d * N_HEADS + h
                spread = jnp.dot(gates, spread_ref[j], preferred_element_type=F32)
                c_b = spread[:, :CHUNK]
                m_prev = LOG2E * m_rows[d][j:j + 1, :]
                w_d = jnp.exp2(jnp.where(masks[d], grow_ref[R_ROW + j:R_ROW + j + 1, rows] - c_b, -jnp.inf))
                s_w = (s_qk * w_d).astype(BF16)
                q_w = (qh32 * jnp.exp2(m_prev - c_b)).astype(BF16)
                tot = jnp.dot(jnp.concatenate([s_w, q_w], axis=1),
                              jnp.concatenate([v_aug, states[d]], axis=0),
                              preferred_element_type=F32)
                floor = jnp.exp2(-spread[:, CHUNK:])
                h_dir = tot[:, :HEAD_DIM] / jnp.maximum(jnp.abs(tot[:, HEAD_DIM:]), floor)
                h_sum = h_dir if h_sum is None else h_sum + h_dir
            mu = jnp.mean(h_sum, axis=-1, keepdims=True)
            hc = h_sum - mu
            hn = hc * lax.rsqrt(jnp.mean(hc * hc, axis=-1, keepdims=True) + LN_EPS)
            hn = hn * nw_ref[:, hs]
            out_ref[rows, hs] = (jax.nn.sigmoid(o_ref[rows, hs].astype(F32)) * hn).astype(BF16)


def _mixer_call(pm, kt, grow, gcol, stf, mf, norm_w, last_chunks):
    t = pm.shape[0]
    n = t // MIX_TILE
    n_sub = MIX_TILE // CHUNK
    rev = lambda i: n - 1 - i
    blk = lambda col: pl.BlockSpec((MIX_TILE, COL_BLOCK), lambda i: (rev(i), col))
    return pl.pallas_call(
        functools.partial(_mixer_kernel, last_chunks),
        grid=(n,),
        in_specs=[blk(Q_BLK), blk(V_BLK), blk(O_BLK),
                  pl.BlockSpec((COL_BLOCK, MIX_TILE), lambda i: (0, rev(i))),
                  pl.BlockSpec((GROW_ROWS, MIX_TILE), lambda i: (0, rev(i))),
                  pl.BlockSpec((MIX_TILE, GCOL_COLS), lambda i: (rev(i), 0)),
                  pl.BlockSpec((n_sub,) + STATE_SHAPE, lambda i: (rev(i), 0, 0, 0)),
                  pl.BlockSpec((n_sub,) + M_SHAPE, lambda i: (rev(i), 0, 0)),
                  _const_spec((N_HD, GCOL_COLS, 2 * CHUNK)),
                  _const_spec((1, MLSTM_WIDTH))],
        out_specs=pl.BlockSpec((MIX_TILE, MLSTM_WIDTH), lambda i: (rev(i), 0)),
        out_shape=jax.ShapeDtypeStruct((t, MLSTM_WIDTH), BF16),
        scratch_shapes=[pltpu.VMEM(STATE_SHAPE, F32), pltpu.VMEM((N_HD, CHUNK), F32),
                        pltpu.VMEM((n_sub,) + STATE_SHAPE, BF16), pltpu.VMEM((n_sub, N_HD, CHUNK), F32)],
        compiler_params=pltpu.CompilerParams(
            dimension_semantics=("arbitrary",), vmem_limit_bytes=VMEM_LIMIT),
        name="mixer",
    )(pm, pm, pm, kt, grow, gcol, stf, mf, _spread_matrix(), norm_w)


def _spread_matrix():
    term = jnp.arange(GCOL_COLS)[None, :, None] // N_HD
    pair = jnp.arange(GCOL_COLS)[None, :, None] % N_HD
    want = jnp.arange(N_HD)[:, None, None]
    lane = jnp.arange(2 * CHUNK)[None, None, :]
    hit = (pair == want) & ((term < C_TERMS) | (lane >= CHUNK))
    return hit.astype(BF16)


def _dense_kernel(tile0, first_subs, last_subs,
                  x_ref, hm_ref, cb_ref, u_ref, up_ref, un_ref, cw_ref, wo_ref,
                  g1_ref, b1_ref, w1_ref, w2_ref, g2_ref, b2_ref, out_ref):
    slab = D_FF // FF_SPLIT
    n_sub = DENSE_TILE // DENSE_SUB
    rows = [slice(r * DENSE_SUB, (r + 1) * DENSE_SUB) for r in range(n_sub)]
    tok = lax.broadcasted_iota(jnp.int32, (DENSE_SUB, CONV_WIDTH), 0)

    def conv_gate(r):
        group = (pl.program_id(0) + tile0) * n_sub + r
        u = u_ref[rows[r], :].astype(F32)
        before = (up_ref[HALO_ROWS - 1:HALO_ROWS, :] if r == 0
                  else u_ref[r * DENSE_SUB - 1:r * DENSE_SUB, :]).astype(F32)
        after = (un_ref[0:1, :] if r == n_sub - 1
                 else u_ref[(r + 1) * DENSE_SUB:(r + 1) * DENSE_SUB + 1, :]).astype(F32)
        before = jnp.where(_is_one_of(group, first_subs), 0.0, before)
        after = jnp.where(_is_one_of(group, last_subs), 0.0, after)
        u_prev = jnp.where(tok == 0, before, pltpu.roll(u, 1, axis=0))
        u_next = jnp.where(tok == DENSE_SUB - 1, after, pltpu.roll(u, DENSE_SUB - 1, axis=0))
        y = cw_ref[0:1, :] * u_prev + cw_ref[1:2, :] * u + cw_ref[2:3, :] * u_next
        return (cb_ref[rows[r], :].astype(F32) * y).astype(BF16)

    def mix_out(r):
        y = jnp.dot(hm_ref[rows[r], :], wo_ref[:MLSTM_WIDTH, :], preferred_element_type=F32)
        return y + jnp.dot(conv_gate(r), wo_ref[MLSTM_WIDTH:, :], preferred_element_type=F32)

    def norm1(r, y):
        x1 = _layer_norm(ALPHA * x_ref[rows[r], :] + y, g1_ref[...], b1_ref[...])
        return x1.astype(BF16), ALPHA * x1

    def ffn_slab(j, x1b, acc):
        hid = jnp.dot(x1b, w1_ref[:, j * slab:(j + 1) * slab], preferred_element_type=F32)
        hid = jnp.square(jnp.maximum(hid, 0.0)).astype(BF16)
        return acc + jnp.dot(hid, w2_ref[j * slab:(j + 1) * slab, :], preferred_element_type=F32)

    state = [None] * n_sub
    state[0] = norm1(0, mix_out(0))
    for r in range(n_sub):
        x1b, acc = state[r]
        acc = ffn_slab(0, x1b, acc)
        if r + 1 < n_sub:
            state[r + 1] = norm1(r + 1, mix_out(r + 1))
        if r > 0:
            out_ref[rows[r - 1], :] = _layer_norm(state[r - 1], g2_ref[...], b2_ref[...])
        for j in range(1, FF_SPLIT):
            acc = ffn_slab(j, x1b, acc)
        state[r] = acc
    out_ref[rows[n_sub - 1], :] = _layer_norm(state[n_sub - 1], g2_ref[...], b2_ref[...])


def _dense_call(x_parts, hm, pm, conv_w, wo, g1, b1, w1, w2, g2, b2, first_subs, last_subs,
                tile_range=None):
    t = hm.shape[0]
    tile0, n_tiles = tile_range if tile_range is not None else (0, t // DENSE_TILE)
    n_first = x_parts[0].shape[0] // DENSE_TILE
    per_halo = DENSE_TILE // HALO_ROWS
    n_halo = t // HALO_ROWS
    row = lambda: pl.BlockSpec((DENSE_TILE, D_MODEL), lambda i: (i + tile0, 0))
    blk = lambda col: pl.BlockSpec((DENSE_TILE, COL_BLOCK), lambda i: (i + tile0, col))
    vec = lambda: _const_spec((1, D_MODEL))
    body = functools.partial(_dense_kernel, tile0, first_subs, last_subs)
    if len(x_parts) == 1:
        x_specs = [row()]
    else:
        assert tile0 == 0
        x_specs = [pl.BlockSpec((DENSE_TILE, D_MODEL), lambda i: (jnp.minimum(i, n_first - 1), 0)),
                   pl.BlockSpec((DENSE_TILE, D_MODEL), lambda i: (jnp.maximum(i - n_first, 0), 0))]
        body = functools.partial(_two_source, body, n_first)
    return pl.pallas_call(
        body,
        grid=(n_tiles,),
        in_specs=x_specs + [
            pl.BlockSpec((DENSE_TILE, MLSTM_WIDTH), lambda i: (i + tile0, 0)),
            blk(CB_BLK), blk(U_BLK),
            pl.BlockSpec((HALO_ROWS, COL_BLOCK),
                         lambda i: (jnp.maximum((i + tile0) * per_halo - 1, 0), U_BLK)),
            pl.BlockSpec((HALO_ROWS, COL_BLOCK),
                         lambda i: (jnp.minimum((i + tile0 + 1) * per_halo, n_halo - 1), U_BLK)),
            _const_spec((3, CONV_WIDTH)),
            _const_spec((D_MODEL, D_MODEL)), vec(), vec(),
            _const_spec((D_MODEL, D_FF)), _const_spec((D_FF, D_MODEL)), vec(), vec()],
        out_specs=pl.BlockSpec((DENSE_TILE, D_MODEL), lambda i: (i, 0)),
        out_shape=jax.ShapeDtypeStruct((n_tiles * DENSE_TILE, D_MODEL), F32),
        compiler_params=pltpu.CompilerParams(
            dimension_semantics=("parallel",), vmem_limit_bytes=VMEM_LIMIT),
        name="dense",
    )(*x_parts, hm, pm, pm, pm, pm, conv_w, wo, g1, b1, w1, w2, g2, b2)


def kernel(x_prompt, x_sample, w_in, b_gate, mh_norm_w, conv_w, w_out, ln1_g, ln1_b, w_ff1, w_ff2, ln2_g, ln2_b):
    seq_lens = [x_prompt.shape[1]] * x_prompt.shape[0] + [x_sample.shape[1]] * x_sample.shape[0]
    assert all(s % ROW_TILE == 0 and s % DENSE_SUB == 0 for s in seq_lens)
    assert sum(seq_lens) % MIX_TILE == 0 and sum(seq_lens) % DENSE_TILE == 0
    starts = [sum(seq_lens[:i]) for i in range(len(seq_lens))]
    first_chunks = tuple(s // CHUNK for s in starts)
    last_chunks = tuple((s + n) // CHUNK - 1 for s, n in zip(starts, seq_lens))
    first_subs = tuple(s // DENSE_SUB for s in starts)
    last_subs = tuple((s + n) // DENSE_SUB - 1 for s, n in zip(starts, seq_lens))

    x_parts = (x_prompt.reshape(-1, D_MODEL), x_sample.reshape(-1, D_MODEL))
    n_prompt = x_parts[0].shape[0]

    gate_lo = 4 * MLSTM_WIDTH
    gate_hi = gate_lo + 2 * N_HD
    assert w_in.shape[-1] == gate_hi + 3 * CONV_WIDTH
    w_in_t = jnp.swapaxes(w_in, 1, 2)
    for l in range(DEPTH):
        w_gate = (w_in_t[l, gate_lo:gate_hi].reshape(N_DIR, 2, N_HEADS, D_MODEL)
                  .transpose(1, 0, 2, 3).reshape(2 * N_HD, D_MODEL).astype(BF16))
        w_conv = w_in_t[l, gate_hi:].astype(BF16)
        bias = b_gate[l].astype(F32).transpose(1, 0, 2).reshape(2 * N_HD, 1)
        pm, kt, grow, gcol, stf, mf, wo_b, w1_b, w2_b = _proj_call(
            x_parts, l, w_in_t, w_gate, w_conv, bias, w_out, w_ff1, w_ff2, first_chunks)
        hm = _mixer_call(pm, kt, grow, gcol, stf, mf, mh_norm_w[l].astype(F32).reshape(1, -1), last_chunks)
        dense = functools.partial(
            _dense_call, x_parts, hm, pm, conv_w[l].astype(F32), wo_b,
            ln1_g[l].reshape(1, -1), ln1_b[l].reshape(1, -1), w1_b, w2_b,
            ln2_g[l].reshape(1, -1), ln2_b[l].reshape(1, -1), first_subs, last_subs)
        if l + 1 < DEPTH or len(x_parts) != 1 or n_prompt % DENSE_TILE:
            x_parts = (dense(),)
        else:
            n_tiles = hm.shape[0] // DENSE_TILE
            split = n_prompt // DENSE_TILE
            return (dense(tile_range=(0, split)).reshape(x_prompt.shape),
                    dense(tile_range=(split, n_tiles - split)).reshape(x_sample.shape))

    x = x_parts[0]
    return (x[:n_prompt].reshape(x_prompt.shape), x[n_prompt:].reshape(x_sample.shape))
```

```python
import functools

import jax
import jax.numpy as jnp
from jax import lax
from jax.experimental import pallas as pl
from jax.experimental.pallas import tpu as pltpu

D_MODEL = 1024
DEPTH = 4
N_HEADS = 4
HEAD_DIM = 128
MLSTM_WIDTH = N_HEADS * HEAD_DIM
CONV_WIDTH = D_MODEL - MLSTM_WIDTH
N_DIR = 2
N_HD = N_DIR * N_HEADS
D_FF = 4 * D_MODEL
ALPHA = (2.0 * DEPTH) ** 0.25
LN_EPS = 1e-5
Q_SCALE = HEAD_DIM ** -0.5
LOG2E = 1.4426950408889634

LANES = 128
COL_BLOCK = 512
Q_BLK, V_BLK, O_BLK, CB_BLK, U_BLK = range(5)
PM_WIDTH = 5 * COL_BLOCK
NT_DIMS = (((1,), (1,)), ((), ()))

CHUNK = 128
PROJ_TILE = 1024
PROJ_TILE_TWO_SOURCE = 512
MIX_TILE = 2048
FF_SPLIT = 4
DENSE_TILE = 1024
DENSE_SUB = 512
BF16_ROWS = 16
HALO_ROWS = BF16_ROWS
VMEM_LIMIT = 62 * 1024 * 1024

R_ROW, A_ROW, G_ROW, AMAX_ROW = 0, N_HD, 2 * N_HD, 3 * N_HD
GROW_ROWS = 4 * N_HD
C_TERMS = 2
B_TERMS = 3
GCOL_COLS = (C_TERMS + B_TERMS) * N_HD
STATE_SHAPE = (N_HEADS, HEAD_DIM, 2 * HEAD_DIM)
M_SHAPE = (2 * N_HD, CHUNK)

F32 = jnp.float32
BF16 = jnp.bfloat16


def _const_spec(shape):
    nd = len(shape)
    return pl.BlockSpec(shape, lambda *_: (0,) * nd, pipeline_mode=pl.Buffered(1))


def _layer_norm(x, g, b):
    mu = jnp.mean(x, axis=-1, keepdims=True)
    xc = x - mu
    var = jnp.mean(xc * xc, axis=-1, keepdims=True)
    return xc * lax.rsqrt(var + LN_EPS) * g + b


def _log_sigmoid(x):
    return jnp.minimum(x, 0.0) - jnp.log1p(jnp.exp(-jnp.abs(x)))


def _is_one_of(idx, values):
    hit = idx == values[0]
    for v in values[1:]:
        hit = jnp.logical_or(hit, idx == v)
    return hit


def _gate_recurrence(a, g, amax, reset, m_scr):
    m_prev = jnp.where(reset, 0.0, m_scr[...])
    m_new = jnp.maximum(g + m_prev, amax)
    m_scr[...] = m_new
    return m_prev, jnp.exp(a - m_new), jnp.where(reset, 0.0, jnp.exp(g + m_prev - m_new))


def _pair_on_lanes(m_rows):
    pair_row = lax.broadcasted_iota(jnp.int32, (N_HD, CHUNK), 0)
    diag = pair_row == jnp.bitwise_and(lax.broadcasted_iota(jnp.int32, (N_HD, CHUNK), 1), N_HD - 1)
    return jnp.sum(jnp.where(diag, m_rows, 0.0), axis=0, keepdims=True)


def _advance_state(s_prev, kt_f32, wa_row, wc_row, v_aug):
    upd = jnp.dot((kt_f32 * wa_row).astype(BF16), v_aug, preferred_element_type=F32)
    return jnp.concatenate([wc_row, wc_row], axis=1) * s_prev + upd


def _chunk_scan(x, combine, neutral, fwd_row, pos):
    width = x.shape[1]
    s = 1
    while s < CHUNK:
        from_left = pltpu.roll(x, s, axis=1)
        from_right = pltpu.roll(x, width - s, axis=1)
        take_left = jnp.logical_and(fwd_row, pos >= s)
        take_right = jnp.logical_and(jnp.logical_not(fwd_row), pos < CHUNK - s)
        x = combine(x, jnp.where(take_left, from_left, jnp.where(take_right, from_right, neutral)))
        s *= 2
    return x


def _proj_kernel(tile, first_chunks,
                 x_ref, wg_ref, wq_ref, wk_ref, wvo_ref, wc_ref, bias_ref, wo_ref, w1_ref, w2_ref,
                 pm_ref, kt_ref, grow_ref, gcol_ref, stf_ref, mf_ref, wo_out_ref, w1_out_ref, w2_out_ref,
                 s_scr, m_scr):
    i = pl.program_id(0)
    n_sub = tile // CHUNK

    @pl.when(i == 0)
    def _():
        s_scr[...] = jnp.zeros_like(s_scr)
        m_scr[...] = jnp.zeros_like(m_scr)

    wo_out_ref[...] = wo_ref[...].astype(BF16)
    w1_out_ref[...] = w1_ref[...].astype(BF16)
    w2_out_ref[...] = w2_ref[...].astype(BF16)
    xb = x_ref[...].astype(BF16)
    g_row = lax.dot_general(wg_ref[...], xb, NT_DIMS, preferred_element_type=F32) + bias_ref[...]
    kt = lax.dot_general(wk_ref[...].astype(BF16), xb, NT_DIMS, preferred_element_type=F32)
    kt_ref[...] = kt.astype(BF16)
    vo = lax.dot_general(xb, wvo_ref[...].astype(BF16), NT_DIMS, preferred_element_type=F32)
    vo = vo.astype(BF16)
    pm_ref[:, V_BLK * COL_BLOCK:CB_BLK * COL_BLOCK] = vo

    ig = g_row[:N_HD, :]
    lf = _log_sigmoid(g_row[N_HD:, :])
    fwd_row = lax.broadcasted_iota(jnp.int32, (N_HD, tile), 0) < N_HEADS
    pos = jnp.bitwise_and(lax.broadcasted_iota(jnp.int32, (N_HD, tile), 1), CHUNK - 1)
    b = _chunk_scan(lf, jnp.add, 0.0, fwd_row, pos)
    r = ig - b
    cm = _chunk_scan(r, jnp.maximum, -jnp.inf, fwd_row, pos)
    at_end = pos == jnp.where(fwd_row, CHUNK - 1, 0)
    b_end = jnp.where(at_end, b, 0.0)
    g_parts, amax_parts = [], []
    for j in range(n_sub):
        sl = slice(j * CHUNK, (j + 1) * CHUNK)
        g_j = jnp.sum(b_end[:, sl], axis=1, keepdims=True)
        amax_j = g_j + jnp.max(r[:, sl], axis=1, keepdims=True)
        g_parts.append(jnp.broadcast_to(g_j, (N_HD, CHUNK)))
        amax_parts.append(jnp.broadcast_to(amax_j, (N_HD, CHUNK)))
    g_rep = jnp.concatenate(g_parts, axis=1)
    a_all = g_rep + r
    grow_ref[R_ROW:R_ROW + N_HD, :] = r * LOG2E
    grow_ref[A_ROW:A_ROW + N_HD, :] = a_all
    grow_ref[G_ROW:G_ROW + N_HD, :] = g_rep
    grow_ref[AMAX_ROW:AMAX_ROW + N_HD, :] = jnp.concatenate(amax_parts, axis=1)

    terms, rest = [], b * LOG2E
    for _ in range(B_TERMS):
        term = rest.astype(BF16).astype(F32)
        terms.append(term)
        rest = rest - term
    col_src = jnp.concatenate(
        [cm * LOG2E] * C_TERMS + terms + [jnp.zeros((LANES - GCOL_COLS, tile), F32)], axis=0)
    gcol_ref[...] = col_src.T[:, :GCOL_COLS]

    ones = jnp.ones((CHUNK, HEAD_DIM), BF16)
    for n in range(n_sub):
        sl = slice(n * CHUNK, (n + 1) * CHUNK)
        reset = _is_one_of(i * n_sub + n, first_chunks)
        m_prev, wa, wc = _gate_recurrence(a_all[:, sl], g_parts[n], amax_parts[n], reset, m_scr)
        mf_ref[n] = jnp.concatenate(
            [m_prev, jnp.broadcast_to(_pair_on_lanes(m_prev), (N_HD, CHUNK))], axis=0)
        for h in range(N_HEADS):
            hs = slice(h * HEAD_DIM, (h + 1) * HEAD_DIM)
            s_prev = jnp.where(reset, 0.0, s_scr[h])
            stf_ref[n, h] = s_prev.astype(BF16)
            v_aug = jnp.concatenate([vo[sl, hs], ones], axis=1)
            s_scr[h] = _advance_state(s_prev, kt[hs, sl], wa[h:h + 1, :], wc[h:h + 1, :], v_aug)

    q = lax.dot_general(xb, wq_ref[...].astype(BF16), NT_DIMS, preferred_element_type=F32)
    pm_ref[:, :COL_BLOCK] = (q * Q_SCALE).astype(BF16)
    c = lax.dot_general(xb, wc_ref[...], NT_DIMS, preferred_element_type=F32)
    pm_ref[:, CB_BLK * COL_BLOCK:U_BLK * COL_BLOCK] = c[:, :COL_BLOCK].astype(BF16)
    pm_ref[:, U_BLK * COL_BLOCK:] = (c[:, COL_BLOCK:2 * COL_BLOCK] * c[:, 2 * COL_BLOCK:]).astype(BF16)


def _proj_call(x_parts, layer, w_in_t, wg, wc, bias, w_out, w_ff1, w_ff2, first_chunks, tile):
    t = sum(xp.shape[0] for xp in x_parts)
    n_first = x_parts[0].shape[0] // tile
    if len(x_parts) == 1:
        x_specs = [pl.BlockSpec((tile, D_MODEL), lambda i: (i, 0))]
        body = functools.partial(_proj_kernel, tile, first_chunks)
    else:
        x_specs = [pl.BlockSpec((tile, D_MODEL), lambda i: (jnp.minimum(i, n_first - 1), 0)),
                   pl.BlockSpec((tile, D_MODEL), lambda i: (jnp.maximum(i - n_first, 0), 0))]
        body = functools.partial(_two_source, functools.partial(_proj_kernel, tile, first_chunks), n_first)
    w_block = lambda rows, blk: pl.BlockSpec(
        (None, rows, D_MODEL), lambda i: (layer, blk, 0), pipeline_mode=pl.Buffered(1))
    n_steps = t // tile
    cast_in = lambda w: pl.BlockSpec((None, w.shape[1] // n_steps, w.shape[2]), lambda i: (layer, i, 0))
    cast_out = lambda w: pl.BlockSpec((w.shape[1] // n_steps, w.shape[2]), lambda i: (i, 0))
    cast_ws = (w_out, w_ff1, w_ff2)
    assert all(w.shape[1] % (n_steps * BF16_ROWS) == 0 for w in cast_ws)
    return pl.pallas_call(
        body,
        grid=(t // tile,),
        in_specs=x_specs + [
            _const_spec((2 * N_HD, D_MODEL)),
            w_block(COL_BLOCK, 0),
            w_block(COL_BLOCK, 1),
            w_block(2 * COL_BLOCK, 1),
            _const_spec((3 * COL_BLOCK, D_MODEL)),
            _const_spec((2 * N_HD, 1)),
        ] + [cast_in(w) for w in cast_ws],
        out_specs=[
            pl.BlockSpec((tile, PM_WIDTH), lambda i: (i, 0)),
            pl.BlockSpec((COL_BLOCK, tile), lambda i: (0, i)),
            pl.BlockSpec((GROW_ROWS, tile), lambda i: (0, i)),
            pl.BlockSpec((tile, GCOL_COLS), lambda i: (i, 0)),
            pl.BlockSpec((tile // CHUNK,) + STATE_SHAPE, lambda i: (i, 0, 0, 0)),
            pl.BlockSpec((tile // CHUNK,) + M_SHAPE, lambda i: (i, 0, 0)),
        ] + [cast_out(w) for w in cast_ws],
        out_shape=[
            jax.ShapeDtypeStruct((t, PM_WIDTH), BF16),
            jax.ShapeDtypeStruct((COL_BLOCK, t), BF16),
            jax.ShapeDtypeStruct((GROW_ROWS, t), F32),
            jax.ShapeDtypeStruct((t, GCOL_COLS), F32),
            jax.ShapeDtypeStruct((t // CHUNK,) + STATE_SHAPE, BF16),
            jax.ShapeDtypeStruct((t // CHUNK,) + M_SHAPE, F32),
        ] + [jax.ShapeDtypeStruct(w.shape[1:], BF16) for w in cast_ws],
        scratch_shapes=[pltpu.VMEM(STATE_SHAPE, F32), pltpu.VMEM((N_HD, CHUNK), F32)],
        compiler_params=pltpu.CompilerParams(
            dimension_semantics=("arbitrary",), vmem_limit_bytes=VMEM_LIMIT),
        name="proj",
    )(*x_parts, wg, w_in_t, w_in_t, w_in_t, wc, bias, *cast_ws)


class _PickedRef:
    def __init__(self, first, second, use_first):
        self._first, self._second, self._use_first = first, second, use_first

    def __getitem__(self, idx):
        return jnp.where(self._use_first, self._first[idx], self._second[idx])


def _two_source(body, n_first, xa_ref, xb_ref, *rest):
    body(_PickedRef(xa_ref, xb_ref, pl.program_id(0) < n_first), *rest)


def _mixer_kernel(last_chunks,
                  q_ref, v_ref, o_ref, kt_ref, grow_ref, gcol_ref, stf_ref, mf_ref, spread_ref, nw_ref,
                  out_ref, s_scr, m_scr, sb_all, mb_all):
    i = pl.program_id(0)
    tile = pl.num_programs(0) - 1 - i

    @pl.when(i == 0)
    def _():
        s_scr[...] = jnp.zeros_like(s_scr)
        m_scr[...] = jnp.zeros_like(m_scr)

    row = lax.broadcasted_iota(jnp.int32, (CHUNK, CHUNK), 0)
    col = lax.broadcasted_iota(jnp.int32, (CHUNK, CHUNK), 1)
    masks = (col <= row, col >= row)
    ones = jnp.ones((CHUNK, HEAD_DIM), BF16)
    gate_lane = lax.broadcasted_iota(jnp.int32, (CHUNK, GCOL_COLS), 1)
    pair_lane = lax.broadcasted_iota(jnp.int32, (1, GCOL_COLS), 1)
    n_sub = MIX_TILE // CHUNK

    for sub in reversed(range(n_sub)):
        rows = slice(sub * CHUNK, (sub + 1) * CHUNK)
        reset = _is_one_of(tile * n_sub + sub, last_chunks)
        mb_prev, wa, wc = _gate_recurrence(
            grow_ref[A_ROW:A_ROW + N_HD, rows], grow_ref[G_ROW:G_ROW + N_HD, rows],
            grow_ref[AMAX_ROW:AMAX_ROW + N_HD, rows], reset, m_scr)
        mb_all[sub] = mb_prev
        for h in range(N_HEADS):
            hs = slice(h * HEAD_DIM, (h + 1) * HEAD_DIM)
            jb = N_HEADS + h
            s_old = s_scr[h]
            sb_all[sub, h] = jnp.where(reset, 0.0, s_old.astype(BF16))
            v_aug = jnp.concatenate([v_ref[rows, hs], ones], axis=1)
            s_scr[h] = _advance_state(s_old, kt_ref[hs, rows].astype(F32),
                                      wa[jb:jb + 1, :], wc[jb:jb + 1, :], v_aug)

    for sub in range(n_sub):
        rows = slice(sub * CHUNK, (sub + 1) * CHUNK)
        mb_prev = mb_all[sub]
        m_rows = (mf_ref[sub, :N_HD, :], mb_prev)
        m_lane = LOG2E * jnp.where(jnp.bitwise_and(pair_lane, N_HD - 1) < N_HEADS,
                                   mf_ref[sub, N_HD:N_HD + 1, :GCOL_COLS],
                                   _pair_on_lanes(mb_prev)[:, :GCOL_COLS])
        gates = gcol_ref[rows, :]
        c_t = jnp.maximum(gates, m_lane)
        c_hi = c_t.astype(BF16).astype(F32)
        gates = jnp.where(gate_lane < N_HD, c_hi,
                          jnp.where(gate_lane < C_TERMS * N_HD, c_t - c_hi, gates)).astype(BF16)
        for h in range(N_HEADS):
            hs = slice(h * HEAD_DIM, (h + 1) * HEAD_DIM)
            qh = q_ref[rows, hs]
            qh32 = qh.astype(F32)
            s_qk = jnp.dot(qh, kt_ref[hs, rows], preferred_element_type=F32)
            v_aug = jnp.concatenate([v_ref[rows, hs], ones], axis=1)
            states = (stf_ref[sub, h], sb_all[sub, h])
            h_sum = None
            for d in range(N_DIR):
                j = d * N_HEADS + h
                spread = jnp.dot(gates, spread_ref[j], preferred_element_type=F32)
                c_b = spread[:, :CHUNK]
                m_prev = LOG2E * m_rows[d][j:j + 1, :]
                w_d = jnp.exp2(jnp.where(masks[d], grow_ref[R_ROW + j:R_ROW + j + 1, rows] - c_b, -jnp.inf))
                s_w = (s_qk * w_d).astype(BF16)
                q_w = (qh32 * jnp.exp2(m_prev - c_b)).astype(BF16)
                tot = jnp.dot(jnp.concatenate([s_w, q_w], axis=1),
                              jnp.concatenate([v_aug, states[d]], axis=0),
                              preferred_element_type=F32)
                floor = jnp.exp2(-spread[:, CHUNK:])
                h_dir = tot[:, :HEAD_DIM] / jnp.maximum(jnp.abs(tot[:, HEAD_DIM:]), floor)
                h_sum = h_dir if h_sum is None else h_sum + h_dir
            mu = jnp.mean(h_sum, axis=-1, keepdims=True)
            hc = h_sum - mu
            hn = hc * lax.rsqrt(jnp.mean(hc * hc, axis=-1, keepdims=True) + LN_EPS)
            hn = hn * nw_ref[:, hs]
            out_ref[rows, hs] = (jax.nn.sigmoid(o_ref[rows, hs].astype(F32)) * hn).astype(BF16)


def _mixer_call(pm, kt, grow, gcol, stf, mf, norm_w, last_chunks):
    t = pm.shape[0]
    n = t // MIX_TILE
    n_sub = MIX_TILE // CHUNK
    rev = lambda i: n - 1 - i
    blk = lambda col: pl.BlockSpec((MIX_TILE, COL_BLOCK), lambda i: (rev(i), col))
    return pl.pallas_call(
        functools.partial(_mixer_kernel, last_chunks),
        grid=(n,),
        in_specs=[blk(Q_BLK), blk(V_BLK), blk(O_BLK),
                  pl.BlockSpec((COL_BLOCK, MIX_TILE), lambda i: (0, rev(i))),
                  pl.BlockSpec((GROW_ROWS, MIX_TILE), lambda i: (0, rev(i))),
                  pl.BlockSpec((MIX_TILE, GCOL_COLS), lambda i: (rev(i), 0)),
                  pl.BlockSpec((n_sub,) + STATE_SHAPE, lambda i: (rev(i), 0, 0, 0)),
                  pl.BlockSpec((n_sub,) + M_SHAPE, lambda i: (rev(i), 0, 0)),
                  _const_spec((N_HD, GCOL_COLS, 2 * CHUNK)),
                  _const_spec((1, MLSTM_WIDTH))],
        out_specs=pl.BlockSpec((MIX_TILE, MLSTM_WIDTH), lambda i: (rev(i), 0)),
        out_shape=jax.ShapeDtypeStruct((t, MLSTM_WIDTH), BF16),
        scratch_shapes=[pltpu.VMEM(STATE_SHAPE, F32), pltpu.VMEM((N_HD, CHUNK), F32),
                        pltpu.VMEM((n_sub,) + STATE_SHAPE, BF16), pltpu.VMEM((n_sub, N_HD, CHUNK), F32)],
        compiler_params=pltpu.CompilerParams(
            dimension_semantics=("arbitrary",), vmem_limit_bytes=VMEM_LIMIT),
        name="mixer",
    )(pm, pm, pm, kt, grow, gcol, stf, mf, _spread_matrix(), norm_w)


def _spread_matrix():
    term = jnp.arange(GCOL_COLS)[None, :, None] // N_HD
    pair = jnp.arange(GCOL_COLS)[None, :, None] % N_HD
    want = jnp.arange(N_HD)[:, None, None]
    lane = jnp.arange(2 * CHUNK)[None, None, :]
    hit = (pair == want) & ((term < C_TERMS) | (lane >= CHUNK))
    return hit.astype(BF16)


def _dense_kernel(tile0, first_subs, last_subs,
                  x_ref, hm_ref, cb_ref, u_ref, up_ref, un_ref, cw_ref, wo_ref,
                  g1_ref, b1_ref, w1_ref, w2_ref, g2_ref, b2_ref, out_ref):
    slab = D_FF // FF_SPLIT
    n_sub = DENSE_TILE // DENSE_SUB
    rows = [slice(r * DENSE_SUB, (r + 1) * DENSE_SUB) for r in range(n_sub)]
    tok = lax.broadcasted_iota(jnp.int32, (DENSE_SUB, CONV_WIDTH), 0)

    def conv_gate(r):
        group = (pl.program_id(0) + tile0) * n_sub + r
        u = u_ref[rows[r], :].astype(F32)
        before = (up_ref[HALO_ROWS - 1:HALO_ROWS, :] if r == 0
                  else u_ref[r * DENSE_SUB - 1:r * DENSE_SUB, :]).astype(F32)
        after = (un_ref[0:1, :] if r == n_sub - 1
                 else u_ref[(r + 1) * DENSE_SUB:(r + 1) * DENSE_SUB + 1, :]).astype(F32)
        before = jnp.where(_is_one_of(group, first_subs), 0.0, before)
        after = jnp.where(_is_one_of(group, last_subs), 0.0, after)
        u_prev = jnp.where(tok == 0, before, pltpu.roll(u, 1, axis=0))
        u_next = jnp.where(tok == DENSE_SUB - 1, after, pltpu.roll(u, DENSE_SUB - 1, axis=0))
        y = cw_ref[0:1, :] * u_prev + cw_ref[1:2, :] * u + cw_ref[2:3, :] * u_next
        return (cb_ref[rows[r], :].astype(F32) * y).astype(BF16)

    def mix_out(r):
        y = jnp.dot(hm_ref[rows[r], :], wo_ref[:MLSTM_WIDTH, :], preferred_element_type=F32)
        return y + jnp.dot(conv_gate(r), wo_ref[MLSTM_WIDTH:, :], preferred_element_type=F32)

    def norm1(r, y):
        x1 = _layer_norm(ALPHA * x_ref[rows[r], :] + y, g1_ref[...], b1_ref[...])
        return x1.astype(BF16), ALPHA * x1

    def ffn_slab(j, x1b, acc):
        hid = jnp.dot(x1b, w1_ref[:, j * slab:(j + 1) * slab], preferred_element_type=F32)
        hid = jnp.square(jnp.maximum(hid, 0.0)).astype(BF16)
        return acc + jnp.dot(hid, w2_ref[j * slab:(j + 1) * slab, :], preferred_element_type=F32)

    state = [None] * n_sub
    state[0] = norm1(0, mix_out(0))
    for r in range(n_sub):
        x1b, acc = state[r]
        acc = ffn_slab(0, x1b, acc)
        if r + 1 < n_sub:
            state[r + 1] = norm1(r + 1, mix_out(r + 1))
        if r > 0:
            out_ref[rows[r - 1], :] = _layer_norm(state[r - 1], g2_ref[...], b2_ref[...])
        for j in range(1, FF_SPLIT):
            acc = ffn_slab(j, x1b, acc)
        state[r] = acc
    out_ref[rows[n_sub - 1], :] = _layer_norm(state[n_sub - 1], g2_ref[...], b2_ref[...])


def _dense_call(x_parts, hm, pm, conv_w, wo, g1, b1, w1, w2, g2, b2, first_subs, last_subs,
                tile_range=None):
    t = hm.shape[0]
    tile0, n_tiles = tile_range if tile_range is not None else (0, t // DENSE_TILE)
    n_first = x_parts[0].shape[0] // DENSE_TILE
    per_halo = DENSE_TILE // HALO_ROWS
    n_halo = t // HALO_ROWS
    row = lambda: pl.BlockSpec((DENSE_TILE, D_MODEL), lambda i: (i + tile0, 0))
    blk = lambda col: pl.BlockSpec((DENSE_TILE, COL_BLOCK), lambda i: (i + tile0, col))
    vec = lambda: _const_spec((1, D_MODEL))
    body = functools.partial(_dense_kernel, tile0, first_subs, last_subs)
    if len(x_parts) == 1:
        x_specs = [row()]
    else:
        assert tile0 == 0
        x_specs = [pl.BlockSpec((DENSE_TILE, D_MODEL), lambda i: (jnp.minimum(i, n_first - 1), 0)),
                   pl.BlockSpec((DENSE_TILE, D_MODEL), lambda i: (jnp.maximum(i - n_first, 0), 0))]
        body = functools.partial(_two_source, body, n_first)
    return pl.pallas_call(
        body,
        grid=(n_tiles,),
        in_specs=x_specs + [
            pl.BlockSpec((DENSE_TILE, MLSTM_WIDTH), lambda i: (i + tile0, 0)),
            blk(CB_BLK), blk(U_BLK),
            pl.BlockSpec((HALO_ROWS, COL_BLOCK),
                         lambda i: (jnp.maximum((i + tile0) * per_halo - 1, 0), U_BLK)),
            pl.BlockSpec((HALO_ROWS, COL_BLOCK),
                         lambda i: (jnp.minimum((i + tile0 + 1) * per_halo, n_halo - 1), U_BLK)),
            _const_spec((3, CONV_WIDTH)),
            _const_spec((D_MODEL, D_MODEL)), vec(), vec(),
            _const_spec((D_MODEL, D_FF)), _const_spec((D_FF, D_MODEL)), vec(), vec()],
        out_specs=pl.BlockSpec((DENSE_TILE, D_MODEL), lambda i: (i, 0)),
        out_shape=jax.ShapeDtypeStruct((n_tiles * DENSE_TILE, D_MODEL), F32),
        compiler_params=pltpu.CompilerParams(
            dimension_semantics=("parallel",), vmem_limit_bytes=VMEM_LIMIT),
        name="dense",
    )(*x_parts, hm, pm, pm, pm, pm, conv_w, wo, g1, b1, w1, w2, g2, b2)


def kernel(x_prompt, x_sample, w_in, b_gate, mh_norm_w, conv_w, w_out, ln1_g, ln1_b, w_ff1, w_ff2, ln2_g, ln2_b):
    seq_lens = [x_prompt.shape[1]] * x_prompt.shape[0] + [x_sample.shape[1]] * x_sample.shape[0]
    assert all(s % DENSE_SUB == 0 for s in seq_lens)
    total = sum(seq_lens)
    assert total % MIX_TILE == 0 and total % DENSE_TILE == 0 and total % PROJ_TILE == 0
    assert all(xp.shape[0] * xp.shape[1] % PROJ_TILE_TWO_SOURCE == 0 for xp in (x_prompt, x_sample))
    starts = [sum(seq_lens[:i]) for i in range(len(seq_lens))]
    first_chunks = tuple(s // CHUNK for s in starts)
    last_chunks = tuple((s + n) // CHUNK - 1 for s, n in zip(starts, seq_lens))
    first_subs = tuple(s // DENSE_SUB for s in starts)
    last_subs = tuple((s + n) // DENSE_SUB - 1 for s, n in zip(starts, seq_lens))

    x_parts = (x_prompt.reshape(-1, D_MODEL), x_sample.reshape(-1, D_MODEL))
    n_prompt = x_parts[0].shape[0]

    gate_lo = 4 * MLSTM_WIDTH
    gate_hi = gate_lo + 2 * N_HD
    assert w_in.shape[-1] == gate_hi + 3 * CONV_WIDTH
    w_in_t = jnp.swapaxes(w_in, 1, 2)
    for l in range(DEPTH):
        w_gate = (w_in_t[l, gate_lo:gate_hi].reshape(N_DIR, 2, N_HEADS, D_MODEL)
                  .transpose(1, 0, 2, 3).reshape(2 * N_HD, D_MODEL).astype(BF16))
        w_conv = w_in_t[l, gate_hi:].astype(BF16)
        bias = b_gate[l].astype(F32).transpose(1, 0, 2).reshape(2 * N_HD, 1)
        pm, kt, grow, gcol, stf, mf, wo_b, w1_b, w2_b = _proj_call(
            x_parts, l, w_in_t, w_gate, w_conv, bias, w_out, w_ff1, w_ff2, first_chunks,
            PROJ_TILE if len(x_parts) == 1 else PROJ_TILE_TWO_SOURCE)
        hm = _mixer_call(pm, kt, grow, gcol, stf, mf, mh_norm_w[l].astype(F32).reshape(1, -1), last_chunks)
        dense = functools.partial(
            _dense_call, x_parts, hm, pm, conv_w[l].astype(F32), wo_b,
            ln1_g[l].reshape(1, -1), ln1_b[l].reshape(1, -1), w1_b, w2_b,
            ln2_g[l].reshape(1, -1), ln2_b[l].reshape(1, -1), first_subs, last_subs)
        if l + 1 < DEPTH or len(x_parts) != 1 or n_prompt % DENSE_TILE:
            x_parts = (dense(),)
        else:
            n_tiles = hm.shape[0] // DENSE_TILE
            split = n_prompt // DENSE_TILE
            return (dense(tile_range=(0, split)).reshape(x_prompt.shape),
                    dense(tile_range=(split, n_tiles - split)).reshape(x_sample.shape))

    x = x_parts[0]
    return (x[:n_prompt].reshape(x_prompt.shape), x[n_prompt:].reshape(x_sample.shape))
```

```python
import functools

import jax
import jax.numpy as jnp
from jax import lax
from jax.experimental import pallas as pl
from jax.experimental.pallas import tpu as pltpu

D_MODEL = 1024
DEPTH = 4
N_HEADS = 4
HEAD_DIM = 128
MLSTM_WIDTH = N_HEADS * HEAD_DIM
CONV_WIDTH = D_MODEL - MLSTM_WIDTH
N_DIR = 2
N_HD = N_DIR * N_HEADS
D_FF = 4 * D_MODEL
ALPHA = (2.0 * DEPTH) ** 0.25
LN_EPS = 1e-5
Q_SCALE = HEAD_DIM ** -0.5
LOG2E = 1.4426950408889634

LANES = 128
COL_BLOCK = 512
Q_BLK, V_BLK, O_BLK, CB_BLK, U_BLK = range(5)
PM_WIDTH = 5 * COL_BLOCK
NT_DIMS = (((1,), (1,)), ((), ()))

CHUNK = 128
PROJ_TILE = 1024
PROJ_TILE_TWO_SOURCE = 512
MIX_TILE = 2048
FF_SPLIT = 4
DENSE_TILE = 1024
DENSE_SUB = 512
BF16_ROWS = 16
HALO_ROWS = BF16_ROWS
VMEM_LIMIT = 62 * 1024 * 1024

R_ROW, A_ROW, G_ROW, AMAX_ROW = 0, N_HD, 2 * N_HD, 3 * N_HD
GROW_ROWS = 4 * N_HD
C_TERMS = 2
B_TERMS = 3
GCOL_COLS = (C_TERMS + B_TERMS) * N_HD
STATE_SHAPE = (N_HEADS, HEAD_DIM, 2 * HEAD_DIM)
M_SHAPE = (2 * N_HD, CHUNK)

F32 = jnp.float32
BF16 = jnp.bfloat16


def _const_spec(shape):
    nd = len(shape)
    return pl.BlockSpec(shape, lambda *_: (0,) * nd, pipeline_mode=pl.Buffered(1))


def _layer_norm(x, g, b):
    mu = jnp.mean(x, axis=-1, keepdims=True)
    xc = x - mu
    var = jnp.mean(xc * xc, axis=-1, keepdims=True)
    return xc * lax.rsqrt(var + LN_EPS) * g + b


def _log_sigmoid(x):
    return jnp.minimum(x, 0.0) - jnp.log1p(jnp.exp(-jnp.abs(x)))


def _is_one_of(idx, values):
    hit = idx == values[0]
    for v in values[1:]:
        hit = jnp.logical_or(hit, idx == v)
    return hit


def _gate_recurrence(a, g, amax, reset, m_scr):
    m_prev = jnp.where(reset, 0.0, m_scr[...])
    m_new = jnp.maximum(g + m_prev, amax)
    m_scr[...] = m_new
    return m_prev, jnp.exp(a - m_new), jnp.where(reset, 0.0, jnp.exp(g + m_prev - m_new))


def _pair_on_lanes(m_rows):
    pair_row = lax.broadcasted_iota(jnp.int32, (N_HD, CHUNK), 0)
    diag = pair_row == jnp.bitwise_and(lax.broadcasted_iota(jnp.int32, (N_HD, CHUNK), 1), N_HD - 1)
    return jnp.sum(jnp.where(diag, m_rows, 0.0), axis=0, keepdims=True)


def _advance_state(s_prev, kt_f32, wa_row, wc_row, v_aug):
    upd = jnp.dot((kt_f32 * wa_row).astype(BF16), v_aug, preferred_element_type=F32)
    return jnp.concatenate([wc_row, wc_row], axis=1) * s_prev + upd


def _chunk_scan(x, combine, neutral, fwd_row, pos):
    width = x.shape[1]
    s = 1
    while s < CHUNK:
        from_left = pltpu.roll(x, s, axis=1)
        from_right = pltpu.roll(x, width - s, axis=1)
        take_left = jnp.logical_and(fwd_row, pos >= s)
        take_right = jnp.logical_and(jnp.logical_not(fwd_row), pos < CHUNK - s)
        x = combine(x, jnp.where(take_left, from_left, jnp.where(take_right, from_right, neutral)))
        s *= 2
    return x


def _proj_kernel(tile, first_chunks,
                 x_ref, wg_ref, wq_ref, wk_ref, wvo_ref, wc_ref, bias_ref, wo_ref, w1_ref, w2_ref,
                 pm_ref, kt_ref, grow_ref, gcol_ref, stf_ref, mf_ref, wo_out_ref, w1_out_ref, w2_out_ref,
                 s_scr, m_scr):
    i = pl.program_id(0)
    n_sub = tile // CHUNK

    @pl.when(i == 0)
    def _():
        s_scr[...] = jnp.zeros_like(s_scr)
        m_scr[...] = jnp.zeros_like(m_scr)

    wo_out_ref[...] = wo_ref[...].astype(BF16)
    w1_out_ref[...] = w1_ref[...].astype(BF16)
    w2_out_ref[...] = w2_ref[...].astype(BF16)
    xb = x_ref[...].astype(BF16)
    g_row = lax.dot_general(wg_ref[...], xb, NT_DIMS, preferred_element_type=F32) + bias_ref[...]
    kt = lax.dot_general(wk_ref[...].astype(BF16), xb, NT_DIMS, preferred_element_type=F32)
    kt_ref[...] = kt.astype(BF16)
    vo = lax.dot_general(xb, wvo_ref[...].astype(BF16), NT_DIMS, preferred_element_type=F32)
    vo = vo.astype(BF16)
    pm_ref[:, V_BLK * COL_BLOCK:CB_BLK * COL_BLOCK] = vo

    ig = g_row[:N_HD, :]
    lf = _log_sigmoid(g_row[N_HD:, :])
    fwd_row = lax.broadcasted_iota(jnp.int32, (N_HD, tile), 0) < N_HEADS
    pos = jnp.bitwise_and(lax.broadcasted_iota(jnp.int32, (N_HD, tile), 1), CHUNK - 1)
    b = _chunk_scan(lf, jnp.add, 0.0, fwd_row, pos)
    r = ig - b
    cm = _chunk_scan(r, jnp.maximum, -jnp.inf, fwd_row, pos)
    at_end = pos == jnp.where(fwd_row, CHUNK - 1, 0)
    b_end = jnp.where(at_end, b, 0.0)
    g_parts, amax_parts = [], []
    for j in range(n_sub):
        sl = slice(j * CHUNK, (j + 1) * CHUNK)
        g_j = jnp.sum(b_end[:, sl], axis=1, keepdims=True)
        amax_j = g_j + jnp.max(r[:, sl], axis=1, keepdims=True)
        g_parts.append(jnp.broadcast_to(g_j, (N_HD, CHUNK)))
        amax_parts.append(jnp.broadcast_to(amax_j, (N_HD, CHUNK)))
    g_rep = jnp.concatenate(g_parts, axis=1)
    a_all = g_rep + r
    grow_ref[R_ROW:R_ROW + N_HD, :] = r * LOG2E
    grow_ref[A_ROW:A_ROW + N_HD, :] = a_all
    grow_ref[G_ROW:G_ROW + N_HD, :] = g_rep
    grow_ref[AMAX_ROW:AMAX_ROW + N_HD, :] = jnp.concatenate(amax_parts, axis=1)

    terms, rest = [], b * LOG2E
    for _ in range(B_TERMS):
        term = rest.astype(BF16).astype(F32)
        terms.append(term)
        rest = rest - term
    col_src = jnp.concatenate(
        [cm * LOG2E] * C_TERMS + terms + [jnp.zeros((LANES - GCOL_COLS, tile), F32)], axis=0)
    gcol_ref[...] = col_src.T[:, :GCOL_COLS]

    ones = jnp.ones((CHUNK, HEAD_DIM), BF16)
    for n in range(n_sub):
        sl = slice(n * CHUNK, (n + 1) * CHUNK)
        reset = _is_one_of(i * n_sub + n, first_chunks)
        m_prev, wa, wc = _gate_recurrence(a_all[:, sl], g_parts[n], amax_parts[n], reset, m_scr)
        mf_ref[n] = jnp.concatenate(
            [m_prev, jnp.broadcast_to(_pair_on_lanes(m_prev), (N_HD, CHUNK))], axis=0)
        for h in range(N_HEADS):
            hs = slice(h * HEAD_DIM, (h + 1) * HEAD_DIM)
            s_prev = jnp.where(reset, 0.0, s_scr[h])
            stf_ref[n, h] = s_prev.astype(BF16)
            v_aug = jnp.concatenate([vo[sl, hs], ones], axis=1)
            s_scr[h] = _advance_state(s_prev, kt[hs, sl], wa[h:h + 1, :], wc[h:h + 1, :], v_aug)

    q = lax.dot_general(xb, wq_ref[...].astype(BF16), NT_DIMS, preferred_element_type=F32)
    pm_ref[:, :COL_BLOCK] = (q * Q_SCALE).astype(BF16)
    c = lax.dot_general(xb, wc_ref[...], NT_DIMS, preferred_element_type=F32)
    pm_ref[:, CB_BLK * COL_BLOCK:U_BLK * COL_BLOCK] = c[:, :COL_BLOCK].astype(BF16)
    pm_ref[:, U_BLK * COL_BLOCK:] = (c[:, COL_BLOCK:2 * COL_BLOCK] * c[:, 2 * COL_BLOCK:]).astype(BF16)


def _proj_call(x_parts, layer, w_in_t, wg, wc, bias, w_out, w_ff1, w_ff2, first_chunks, tile):
    t = sum(xp.shape[0] for xp in x_parts)
    n_first = x_parts[0].shape[0] // tile
    if len(x_parts) == 1:
        x_specs = [pl.BlockSpec((tile, D_MODEL), lambda i: (i, 0))]
        body = functools.partial(_proj_kernel, tile, first_chunks)
    else:
        x_specs = [pl.BlockSpec((tile, D_MODEL), lambda i: (jnp.minimum(i, n_first - 1), 0)),
                   pl.BlockSpec((tile, D_MODEL), lambda i: (jnp.maximum(i - n_first, 0), 0))]
        body = functools.partial(_two_source, functools.partial(_proj_kernel, tile, first_chunks), n_first)
    w_block = lambda rows, blk: pl.BlockSpec(
        (None, rows, D_MODEL), lambda i: (layer, blk, 0), pipeline_mode=pl.Buffered(1))
    n_steps = t // tile
    cast_in = lambda w: pl.BlockSpec((None, w.shape[1] // n_steps, w.shape[2]), lambda i: (layer, i, 0))
    cast_out = lambda w: pl.BlockSpec((w.shape[1] // n_steps, w.shape[2]), lambda i: (i, 0))
    cast_ws = (w_out, w_ff1, w_ff2)
    assert all(w.shape[1] % (n_steps * BF16_ROWS) == 0 for w in cast_ws)
    return pl.pallas_call(
        body,
        grid=(t // tile,),
        in_specs=x_specs + [
            _const_spec((2 * N_HD, D_MODEL)),
            w_block(COL_BLOCK, 0),
            w_block(COL_BLOCK, 1),
            w_block(2 * COL_BLOCK, 1),
            _const_spec((3 * COL_BLOCK, D_MODEL)),
            _const_spec((2 * N_HD, 1)),
        ] + [cast_in(w) for w in cast_ws],
        out_specs=[
            pl.BlockSpec((tile, PM_WIDTH), lambda i: (i, 0)),
            pl.BlockSpec((COL_BLOCK, tile), lambda i: (0, i)),
            pl.BlockSpec((GROW_ROWS, tile), lambda i: (0, i)),
            pl.BlockSpec((tile, GCOL_COLS), lambda i: (i, 0)),
            pl.BlockSpec((tile // CHUNK,) + STATE_SHAPE, lambda i: (i, 0, 0, 0)),
            pl.BlockSpec((tile // CHUNK,) + M_SHAPE, lambda i: (i, 0, 0)),
        ] + [cast_out(w) for w in cast_ws],
        out_shape=[
            jax.ShapeDtypeStruct((t, PM_WIDTH), BF16),
            jax.ShapeDtypeStruct((COL_BLOCK, t), BF16),
            jax.ShapeDtypeStruct((GROW_ROWS, t), F32),
            jax.ShapeDtypeStruct((t, GCOL_COLS), F32),
            jax.ShapeDtypeStruct((t // CHUNK,) + STATE_SHAPE, BF16),
            jax.ShapeDtypeStruct((t // CHUNK,) + M_SHAPE, F32),
        ] + [jax.ShapeDtypeStruct(w.shape[1:], BF16) for w in cast_ws],
        scratch_shapes=[pltpu.VMEM(STATE_SHAPE, F32), pltpu.VMEM((N_HD, CHUNK), F32)],
        compiler_params=pltpu.CompilerParams(
            dimension_semantics=("arbitrary",), vmem_limit_bytes=VMEM_LIMIT),
        name="proj",
    )(*x_parts, wg, w_in_t, w_in_t, w_in_t, wc, bias, *cast_ws)


class _PickedRef:
    def __init__(self, first, second, use_first):
        self._first, self._second, self._use_first = first, second, use_first

    def __getitem__(self, idx):
        return jnp.where(self._use_first, self._first[idx], self._second[idx])


def _two_source(body, n_first, xa_ref, xb_ref, *rest):
    body(_PickedRef(xa_ref, xb_ref, pl.program_id(0) < n_first), *rest)


def _mixer_kernel(last_chunks,
                  q_ref, v_ref, o_ref, kt_ref, grow_ref, gcol_ref, stf_ref, mf_ref, spread_ref, nw_ref,
                  out_ref, s_scr, m_scr, sb_all, mb_all):
    i = pl.program_id(0)
    tile = pl.num_programs(0) - 1 - i

    @pl.when(i == 0)
    def _():
        s_scr[...] = jnp.zeros_like(s_scr)
        m_scr[...] = jnp.zeros_like(m_scr)

    row = lax.broadcasted_iota(jnp.int32, (CHUNK, CHUNK), 0)
    col = lax.broadcasted_iota(jnp.int32, (CHUNK, CHUNK), 1)
    masks = (col <= row, col >= row)
    ones = jnp.ones((CHUNK, HEAD_DIM), BF16)
    gate_lane = lax.broadcasted_iota(jnp.int32, (CHUNK, GCOL_COLS), 1)
    pair_lane = lax.broadcasted_iota(jnp.int32, (1, GCOL_COLS), 1)
    n_sub = MIX_TILE // CHUNK

    for sub in reversed(range(n_sub)):
        rows = slice(sub * CHUNK, (sub + 1) * CHUNK)
        reset = _is_one_of(tile * n_sub + sub, last_chunks)
        mb_prev, wa, wc = _gate_recurrence(
            grow_ref[A_ROW:A_ROW + N_HD, rows], grow_ref[G_ROW:G_ROW + N_HD, rows],
            grow_ref[AMAX_ROW:AMAX_ROW + N_HD, rows], reset, m_scr)
        mb_all[sub] = mb_prev
        for h in range(N_HEADS):
            hs = slice(h * HEAD_DIM, (h + 1) * HEAD_DIM)
            jb = N_HEADS + h
            s_old = s_scr[h]
            sb_all[sub, h] = jnp.where(reset, 0.0, s_old.astype(BF16))
            v_aug = jnp.concatenate([v_ref[rows, hs], ones], axis=1)
            s_scr[h] = _advance_state(s_old, kt_ref[hs, rows].astype(F32),
                                      wa[jb:jb + 1, :], wc[jb:jb + 1, :], v_aug)

    for sub in range(n_sub):
        rows = slice(sub * CHUNK, (sub + 1) * CHUNK)
        mb_prev = mb_all[sub]
        m_rows = (mf_ref[sub, :N_HD, :], mb_prev)
        m_lane = LOG2E * jnp.where(jnp.bitwise_and(pair_lane, N_HD - 1) < N_HEADS,
                                   mf_ref[sub, N_HD:N_HD + 1, :GCOL_COLS],
                                   _pair_on_lanes(mb_prev)[:, :GCOL_COLS])
        gates = gcol_ref[rows, :]
        c_t = jnp.maximum(gates, m_lane)
        c_hi = c_t.astype(BF16).astype(F32)
        gates = jnp.where(gate_lane < N_HD, c_hi,
                          jnp.where(gate_lane < C_TERMS * N_HD, c_t - c_hi, gates)).astype(BF16)
        for h in range(N_HEADS):
            hs = slice(h * HEAD_DIM, (h + 1) * HEAD_DIM)
            qh = q_ref[rows, hs]
            qh32 = qh.astype(F32)
            s_qk = jnp.dot(qh, kt_ref[hs, rows], preferred_element_type=F32)
            v_aug = jnp.concatenate([v_ref[rows, hs], ones], axis=1)
            states = (stf_ref[sub, h], sb_all[sub, h])
            h_sum = None
            for d in range(N_DIR):
                j = d * N_HEADS + h
                spread = jnp.dot(gates, spread_ref[j], preferred_element_type=F32)
                c_b = spread[:, :CHUNK]
                m_prev = LOG2E * m_rows[d][j:j + 1, :]
                w_d = jnp.exp2(jnp.where(masks[d], grow_ref[R_ROW + j:R_ROW + j + 1, rows] - c_b, -jnp.inf))
                s_w = (s_qk * w_d).astype(BF16)
                q_w = (qh32 * jnp.exp2(m_prev - c_b)).astype(BF16)
                tot = jnp.dot(jnp.concatenate([s_w, q_w], axis=1),
                              jnp.concatenate([v_aug, states[d]], axis=0),
                              preferred_element_type=F32)
                floor = jnp.exp2(-spread[:, CHUNK:])
                h_dir = tot[:, :HEAD_DIM] / jnp.maximum(jnp.abs(tot[:, HEAD_DIM:]), floor)
                h_sum = h_dir if h_sum is None else h_sum + h_dir
            mu = jnp.mean(h_sum, axis=-1, keepdims=True)
            hc = h_sum - mu
            hn = hc * lax.rsqrt(jnp.mean(hc * hc, axis=-1, keepdims=True) + LN_EPS)
            hn = hn * nw_ref[:, hs]
            out_ref[rows, hs] = (jax.nn.sigmoid(o_ref[rows, hs].astype(F32)) * hn).astype(BF16)


def _mixer_call(pm, kt, grow, gcol, stf, mf, norm_w, last_chunks):
    t = pm.shape[0]
    n = t // MIX_TILE
    n_sub = MIX_TILE // CHUNK
    rev = lambda i: n - 1 - i
    blk = lambda col: pl.BlockSpec((MIX_TILE, COL_BLOCK), lambda i: (rev(i), col))
    return pl.pallas_call(
        functools.partial(_mixer_kernel, last_chunks),
        grid=(n,),
        in_specs=[blk(Q_BLK), blk(V_BLK), blk(O_BLK),
                  pl.BlockSpec((COL_BLOCK, MIX_TILE), lambda i: (0, rev(i))),
                  pl.BlockSpec((GROW_ROWS, MIX_TILE), lambda i: (0, rev(i))),
                  pl.BlockSpec((MIX_TILE, GCOL_COLS), lambda i: (rev(i), 0)),
                  pl.BlockSpec((n_sub,) + STATE_SHAPE, lambda i: (rev(i), 0, 0, 0)),
                  pl.BlockSpec((n_sub,) + M_SHAPE, lambda i: (rev(i), 0, 0)),
                  _const_spec((N_HD, GCOL_COLS, 2 * CHUNK)),
                  _const_spec((1, MLSTM_WIDTH))],
        out_specs=pl.BlockSpec((MIX_TILE, MLSTM_WIDTH), lambda i: (rev(i), 0)),
        out_shape=jax.ShapeDtypeStruct((t, MLSTM_WIDTH), BF16),
        scratch_shapes=[pltpu.VMEM(STATE_SHAPE, F32), pltpu.VMEM((N_HD, CHUNK), F32),
                        pltpu.VMEM((n_sub,) + STATE_SHAPE, BF16), pltpu.VMEM((n_sub, N_HD, CHUNK), F32)],
        compiler_params=pltpu.CompilerParams(
            dimension_semantics=("arbitrary",), vmem_limit_bytes=VMEM_LIMIT),
        name="mixer",
    )(pm, pm, pm, kt, grow, gcol, stf, mf, _spread_matrix(), norm_w)


def _spread_matrix():
    term = jnp.arange(GCOL_COLS)[None, :, None] // N_HD
    pair = jnp.arange(GCOL_COLS)[None, :, None] % N_HD
    want = jnp.arange(N_HD)[:, None, None]
    lane = jnp.arange(2 * CHUNK)[None, None, :]
    hit = (pair == want) & ((term < C_TERMS) | (lane >= CHUNK))
    return hit.astype(BF16)


def _dense_kernel(tile0, first_subs, last_subs,
                  x_ref, hm_ref, cb_ref, u_ref, up_ref, un_ref, cw_ref, wo_ref,
                  g1_ref, b1_ref, w1_ref, w2_ref, g2_ref, b2_ref, out_ref):
    slab = D_FF // FF_SPLIT
    n_sub = DENSE_TILE // DENSE_SUB
    rows = [slice(r * DENSE_SUB, (r + 1) * DENSE_SUB) for r in range(n_sub)]
    tok = lax.broadcasted_iota(jnp.int32, (DENSE_SUB, CONV_WIDTH), 0)

    def conv_gate(r):
        group = (pl.program_id(0) + tile0) * n_sub + r
        u = u_ref[rows[r], :].astype(F32)
        before = (up_ref[HALO_ROWS - 1:HALO_ROWS, :] if r == 0
                  else u_ref[r * DENSE_SUB - 1:r * DENSE_SUB, :]).astype(F32)
        after = (un_ref[0:1, :] if r == n_sub - 1
                 else u_ref[(r + 1) * DENSE_SUB:(r + 1) * DENSE_SUB + 1, :]).astype(F32)
        before = jnp.where(_is_one_of(group, first_subs), 0.0, before)
        after = jnp.where(_is_one_of(group, last_subs), 0.0, after)
        u_prev = jnp.where(tok == 0, before, pltpu.roll(u, 1, axis=0))
        u_next = jnp.where(tok == DENSE_SUB - 1, after, pltpu.roll(u, DENSE_SUB - 1, axis=0))
        y = cw_ref[0:1, :] * u_prev + cw_ref[1:2, :] * u + cw_ref[2:3, :] * u_next
        return (cb_ref[rows[r], :].astype(F32) * y).astype(BF16)

    def mix_out(r):
        y = jnp.dot(hm_ref[rows[r], :], wo_ref[:MLSTM_WIDTH, :], preferred_element_type=F32)
        return y + jnp.dot(conv_gate(r), wo_ref[MLSTM_WIDTH:, :], preferred_element_type=F32)

    def norm1(r, y):
        x1 = _layer_norm(ALPHA * x_ref[rows[r], :] + y, g1_ref[...], b1_ref[...])
        return x1.astype(BF16), ALPHA * x1

    def ffn_slab(j, x1b, acc):
        hid = jnp.dot(x1b, w1_ref[:, j * slab:(j + 1) * slab], preferred_element_type=F32)
        hid = jnp.square(jnp.maximum(hid, 0.0)).astype(BF16)
        return acc + jnp.dot(hid, w2_ref[j * slab:(j + 1) * slab, :], preferred_element_type=F32)

    state = [None] * n_sub
    state[0] = norm1(0, mix_out(0))
    for r in range(n_sub):
        x1b, acc = state[r]
        acc = ffn_slab(0, x1b, acc)
        if r + 1 < n_sub:
            state[r + 1] = norm1(r + 1, mix_out(r + 1))
        if r > 0:
            out_ref[rows[r - 1], :] = _layer_norm(state[r - 1], g2_ref[...], b2_ref[...])
        for j in range(1, FF_SPLIT):
            acc = ffn_slab(j, x1b, acc)
        state[r] = acc
    out_ref[rows[n_sub - 1], :] = _layer_norm(state[n_sub - 1], g2_ref[...], b2_ref[...])


def _dense_call(x_parts, hm, pm, conv_w, wo, g1, b1, w1, w2, g2, b2, first_subs, last_subs,
                tile_range=None):
    t = hm.shape[0]
    tile0, n_tiles = tile_range if tile_range is not None else (0, t // DENSE_TILE)
    n_first = x_parts[0].shape[0] // DENSE_TILE
    per_halo = DENSE_TILE // HALO_ROWS
    n_halo = t // HALO_ROWS
    row = lambda: pl.BlockSpec((DENSE_TILE, D_MODEL), lambda i: (i + tile0, 0))
    blk = lambda col: pl.BlockSpec((DENSE_TILE, COL_BLOCK), lambda i: (i + tile0, col))
    vec = lambda: _const_spec((1, D_MODEL))
    body = functools.partial(_dense_kernel, tile0, first_subs, last_subs)
    if len(x_parts) == 1:
        x_specs = [row()]
    else:
        assert tile0 == 0
        x_specs = [pl.BlockSpec((DENSE_TILE, D_MODEL), lambda i: (jnp.minimum(i, n_first - 1), 0)),
                   pl.BlockSpec((DENSE_TILE, D_MODEL), lambda i: (jnp.maximum(i - n_first, 0), 0))]
        body = functools.partial(_two_source, body, n_first)
    return pl.pallas_call(
        body,
        grid=(n_tiles,),
        in_specs=x_specs + [
            pl.BlockSpec((DENSE_TILE, MLSTM_WIDTH), lambda i: (i + tile0, 0)),
            blk(CB_BLK), blk(U_BLK),
            pl.BlockSpec((HALO_ROWS, COL_BLOCK),
                         lambda i: (jnp.maximum((i + tile0) * per_halo - 1, 0), U_BLK)),
            pl.BlockSpec((HALO_ROWS, COL_BLOCK),
                         lambda i: (jnp.minimum((i + tile0 + 1) * per_halo, n_halo - 1), U_BLK)),
            _const_spec((3, CONV_WIDTH)),
            _const_spec((D_MODEL, D_MODEL)), vec(), vec(),
            _const_spec((D_MODEL, D_FF)), _const_spec((D_FF, D_MODEL)), vec(), vec()],
        out_specs=pl.BlockSpec((DENSE_TILE, D_MODEL), lambda i: (i, 0)),
        out_shape=jax.ShapeDtypeStruct((n_tiles * DENSE_TILE, D_MODEL), F32),
        compiler_params=pltpu.CompilerParams(
            dimension_semantics=("parallel",), vmem_limit_bytes=VMEM_LIMIT),
        name="dense",
    )(*x_parts, hm, pm, pm, pm, pm, conv_w, wo, g1, b1, w1, w2, g2, b2)


def kernel(x_prompt, x_sample, w_in, b_gate, mh_norm_w, conv_w, w_out, ln1_g, ln1_b, w_ff1, w_ff2, ln2_g, ln2_b):
    seq_lens = [x_prompt.shape[1]] * x_prompt.shape[0] + [x_sample.shape[1]] * x_sample.shape[0]
    assert all(s % DENSE_SUB == 0 for s in seq_lens)
    total = sum(seq_lens)
    assert total % MIX_TILE == 0 and total % DENSE_TILE == 0 and total % PROJ_TILE == 0
    assert all(xp.shape[0] * xp.shape[1] % PROJ_TILE_TWO_SOURCE == 0 for xp in (x_prompt, x_sample))
    starts = [sum(seq_lens[:i]) for i in range(len(seq_lens))]
    first_chunks = tuple(s // CHUNK for s in starts)
    last_chunks = tuple((s + n) // CHUNK - 1 for s, n in zip(starts, seq_lens))
    first_subs = tuple(s // DENSE_SUB for s in starts)
    last_subs = tuple((s + n) // DENSE_SUB - 1 for s, n in zip(starts, seq_lens))

    x_parts = (x_prompt.reshape(-1, D_MODEL), x_sample.reshape(-1, D_MODEL))
    n_prompt = x_parts[0].shape[0]

    gate_lo = 4 * MLSTM_WIDTH
    gate_hi = gate_lo + 2 * N_HD
    assert w_in.shape[-1] == gate_hi + 3 * CONV_WIDTH
    w_in_t = jnp.swapaxes(w_in, 1, 2)
    w_tail = lax.optimization_barrier(w_in_t[:, gate_lo:])
    for l in range(DEPTH):
        w_gate = (w_tail[l, :2 * N_HD].reshape(N_DIR, 2, N_HEADS, D_MODEL)
                  .transpose(1, 0, 2, 3).reshape(2 * N_HD, D_MODEL).astype(BF16))
        w_conv = w_tail[l, 2 * N_HD:].astype(BF16)
        bias = b_gate[l].astype(F32).transpose(1, 0, 2).reshape(2 * N_HD, 1)
        pm, kt, grow, gcol, stf, mf, wo_b, w1_b, w2_b = _proj_call(
            x_parts, l, w_in_t, w_gate, w_conv, bias, w_out, w_ff1, w_ff2, first_chunks,
            PROJ_TILE if len(x_parts) == 1 else PROJ_TILE_TWO_SOURCE)
        hm = _mixer_call(pm, kt, grow, gcol, stf, mf, mh_norm_w[l].astype(F32).reshape(1, -1), last_chunks)
        dense = functools.partial(
            _dense_call, x_parts, hm, pm, conv_w[l].astype(F32), wo_b,
            ln1_g[l].reshape(1, -1), ln1_b[l].reshape(1, -1), w1_b, w2_b,
            ln2_g[l].reshape(1, -1), ln2_b[l].reshape(1, -1), first_subs, last_subs)
        if l + 1 < DEPTH or len(x_parts) != 1 or n_prompt % DENSE_TILE:
            x_parts = (dense(),)
        else:
            n_tiles = hm.shape[0] // DENSE_TILE
            split = n_prompt // DENSE_TILE
            return (dense(tile_range=(0, split)).reshape(x_prompt.shape),
                    dense(tile_range=(split, n_tiles - split)).reshape(x_sample.shape))

    x = x_parts[0]
    return (x[:n_prompt].reshape(x_prompt.shape), x[n_prompt:].reshape(x_sample.shape))
```
